```python
import jax, jax.numpy as jnp
from jax import lax
import numpy as np

D_MODEL = 1024
BATCH = 4
SEQ = 4096
DEPTH = 2
DEC_BATCH = 32
DEC_SEQ = 8
PAST_LEN = 8192
PAGE_SIZE = 128

HEAD_DIM = 64
MIX_WIDTH = D_MODEL
H_A = (MIX_WIDTH // 2) // HEAD_DIM
D_A = H_A * HEAD_DIM
D_B = MIX_WIDTH - D_A
G_B = 8
DG_B = D_B // G_B
H_C = (MIX_WIDTH // 2) // HEAD_DIM
D_C = H_C * HEAD_DIM
D_D = MIX_WIDTH - D_C
CHUNK = 128
QBLOCK = 128
CONV_W = 3
D_FF = 4 * D_MODEL
N_EVEN = (DEPTH + 1) // 2
N_ODD = DEPTH // 2
EPS = 1e-6
FORGET_BIAS = 2.0
EVEN_WIDTHS = (D_A, D_A, D_A, H_A, D_B, D_B)
ODD_WIDTHS = (D_C, D_C, D_C, D_D, D_D, D_D)

kernel_name = "fox_sgu_stickbreak_shortconv_decoder_step"


def _split(p, widths):
    parts, start = [], 0
    for w in widths:
        parts.append(p[..., start:start + w])
        start += w
    return parts


def rms_norm(x, g):
    xf = x.astype(jnp.float32)
    y = xf * lax.rsqrt(jnp.mean(xf * xf, axis=-1, keepdims=True) + EPS) * g.astype(jnp.float32)
    return y.astype(x.dtype)


def layer_norm(x, g, b):
    xf = x.astype(jnp.float32)
    mu = jnp.mean(xf, axis=-1, keepdims=True)
    xc = xf - mu
    var = jnp.mean(xc * xc, axis=-1, keepdims=True)
    y = xc * lax.rsqrt(var + EPS) * g.astype(jnp.float32) + b.astype(jnp.float32)
    return y.astype(x.dtype)


def gather_pages(cache, page_table):
    g = cache[page_table]
    return g.reshape((g.shape[0], g.shape[1] * g.shape[2]) + g.shape[3:])


def sweep_query_blocks(fn, q_args, qpos):
    t = qpos.shape[0]
    nb = t // QBLOCK
    blocks = tuple(jnp.moveaxis(a.reshape((a.shape[0], nb, QBLOCK) + a.shape[2:]), 1, 0) for a in q_args)
    out = lax.map(lambda args: fn(*args[0], args[1]), (blocks, qpos.reshape(nb, QBLOCK)))
    out = jnp.moveaxis(out, 0, 1)
    return out.reshape((out.shape[0], t) + out.shape[3:])


def fox_attend(q, fq, qpos, k, fk, v, kpos):
    s = jnp.einsum("bqhd,bkhd->bhqk", q, k, preferred_element_type=jnp.float32) * (HEAD_DIM ** -0.5)
    s = s + jnp.swapaxes(fq, 1, 2)[..., :, None] - jnp.swapaxes(fk, 1, 2)[..., None, :]
    s = jnp.where(kpos[None, :] <= qpos[:, None], s, -jnp.inf)
    p = jax.nn.softmax(s, axis=-1)
    return jnp.einsum("bhqk,bkhd->bqhd", p.astype(v.dtype), v)


def sb_attend(q, qpos, k, v, kpos):
    z = jnp.einsum("bqhd,bkhd->bhqk", q, k, preferred_element_type=jnp.float32) * (HEAD_DIM ** -0.5)
    mask = kpos[None, :] < qpos[:, None]
    log_keep = jnp.where(mask, jax.nn.log_sigmoid(-z), 0.0)
    later = lax.cumsum(log_keep, axis=3, reverse=True) - log_keep
    a = jnp.where(mask, jnp.exp(jax.nn.log_sigmoid(z) + later), 0.0)
    return jnp.einsum("bhqk,bkhd->bqhd", a.astype(v.dtype), v)


def sgu_spatial(vn, w_s, b_s):
    b, t, _ = vn.shape
    nc = -(-t // CHUNK)
    vp = jnp.pad(vn, ((0, 0), (0, nc * CHUNK - t), (0, 0))).reshape(b, nc, CHUNK, G_B, DG_B)
    mixed = jnp.einsum("gts,bcsgd->bctgd", jnp.tril(w_s), vp) + b_s.T[None, None, :, :, None]
    return mixed.reshape(b, nc * CHUNK, D_B)[:, :t]


def even_mixer(h, w_in, b_forget, sgu_g, sgu_b, w_s, b_s, w_out, past):
    b, t, _ = h.shape
    q, k, v, f_logit, u_g, v_g = _split(h @ w_in, EVEN_WIDTHS)
    q = q.reshape(b, t, H_A, HEAD_DIM)
    k = k.reshape(b, t, H_A, HEAD_DIM)
    v = v.reshape(b, t, H_A, HEAD_DIM)
    logf = jax.nn.log_sigmoid(f_logit.astype(jnp.float32) + b_forget.astype(jnp.float32))
    if past is None:
        kpos = jnp.arange(t)
        qpos = kpos
        k_all, v_all, logf_all = k, v, logf
    else:
        pk, pv, plogf = past
        n_past = pk.shape[1]
        qpos = n_past + jnp.arange(t)
        kpos = jnp.arange(n_past + t)
        k_all = jnp.concatenate([pk.astype(k.dtype), k], axis=1)
        v_all = jnp.concatenate([pv.astype(v.dtype), v], axis=1)
        logf_all = jnp.concatenate([plogf.astype(jnp.float32), logf], axis=1)
    F = jnp.cumsum(logf_all, axis=1)
    Fq = F[:, -t:]
    if past is None:
        a = sweep_query_blocks(lambda qb, fb, pb: fox_attend(qb, fb, pb, k_all, F, v_all, kpos), (q, Fq), qpos)
    else:
        a = fox_attend(q, Fq, qpos, k_all, F, v_all, kpos)
    vn = layer_norm(v_g, sgu_g, sgu_b)
    out_b = u_g * sgu_spatial(vn, w_s, b_s)
    y = jnp.concatenate([a.reshape(b, t, D_A), out_b], axis=-1) @ w_out
    return y, (k, v, logf, vn)


def odd_mixer(h, w_in, conv_w, w_out, past):
    b, t, _ = h.shape
    q, k, v, gate_b, gate_c, h_in = _split(h @ w_in, ODD_WIDTHS)
    q = q.reshape(b, t, H_C, HEAD_DIM)
    k = k.reshape(b, t, H_C, HEAD_DIM)
    v = v.reshape(b, t, H_C, HEAD_DIM)
    u = gate_c * h_in
    if past is None:
        prev = jnp.zeros((b, CONV_W - 1, D_D), u.dtype)
        kpos = jnp.arange(t)
        qpos = kpos
        k_all, v_all = k, v
    else:
        pk, pv, prev = past
        n_past = pk.shape[1]
        qpos = n_past + jnp.arange(t)
        kpos = jnp.arange(n_past + t)
        k_all = jnp.concatenate([pk.astype(k.dtype), k], axis=1)
        v_all = jnp.concatenate([pv.astype(v.dtype), v], axis=1)
    u_ext = jnp.concatenate([prev.astype(u.dtype), u], axis=1)
    conv = conv_w[0] * u_ext[:, 0:t]
    for j in range(1, CONV_W):
        conv = conv + conv_w[j] * u_ext[:, j:j + t]
    out_d = gate_b * conv
    new_conv = u_ext[:, -(CONV_W - 1):]
    if past is None:
        a = sweep_query_blocks(lambda qb, pb: sb_attend(qb, pb, k_all, v_all, kpos), (q,), qpos)
    else:
        a = sb_attend(q, qpos, k_all, v_all, kpos)
    y = jnp.concatenate([a.reshape(b, t, D_C), out_d], axis=-1) @ w_out
    return y, (k, v, new_conv)


def trunk(x, g_mix, g_mlp, g_final, w_up, w_down, w_in_even, b_forget, sgu_g, sgu_b, w_spatial, b_spatial,
          w_out_even, w_in_odd, conv_w, w_out_odd, cache):
    even_rows, odd_rows = [], []
    for layer in range(DEPTH):
        i = layer // 2
        h = rms_norm(x, g_mix[layer])
        if layer % 2 == 0:
            past = None
            if cache is not None:
                pt = cache[0]
                past = (gather_pages(cache[1][i], pt), gather_pages(cache[2][i], pt), gather_pages(cache[3][i], pt))
            mix, rows = even_mixer(h, w_in_even[i], b_forget[i], sgu_g[i], sgu_b[i], w_spatial[i], b_spatial[i],
                                   w_out_even[i], past)
            even_rows.append(rows)
        else:
            past = None
            if cache is not None:
                pt = cache[0]
                past = (gather_pages(cache[4][i], pt), gather_pages(cache[5][i], pt), cache[6][i])
            mix, rows = odd_mixer(h, w_in_odd[i], conv_w[i], w_out_odd[i], past)
            odd_rows.append(rows)
        x = x + mix
        h = rms_norm(x, g_mlp[layer])
        x = x + jnp.square(jax.nn.relu(h @ w_up[layer])) @ w_down[layer]
    y = rms_norm(x, g_final)
    even_new = [jnp.stack(r, axis=0) for r in zip(*even_rows)]
    odd_new = [jnp.stack(r, axis=0) for r in zip(*odd_rows)]
    return y, even_new, odd_new


def setup_inputs(seed: int = 0) -> dict:
    key = jax.random.key(seed)
    ks = jax.random.split(key, 26)
    f32 = jnp.float32
    n_pages = PAST_LEN // PAGE_SIZE
    n_used = DEC_BATCH * n_pages
    n_pool = n_used + max(1, n_used // 4)

    def nrm(k, shape, scale):
        return jax.random.normal(k, shape, f32) * scale

    x_prompt = nrm(ks[0], (BATCH, SEQ, D_MODEL), 1.0)
    x_sample = nrm(ks[1], (DEC_BATCH, DEC_SEQ, D_MODEL), 1.0)
    cache_fox_k = nrm(ks[2], (N_EVEN, n_pool, PAGE_SIZE, H_A, HEAD_DIM), 1.0)
    cache_fox_v = nrm(ks[3], (N_EVEN, n_pool, PAGE_SIZE, H_A, HEAD_DIM), 1.0)
    cache_fox_logf = jax.nn.log_sigmoid(FORGET_BIAS + nrm(ks[4], (N_EVEN, n_pool, PAGE_SIZE, H_A), 1.0))
    cache_sb_k = nrm(ks[5], (N_ODD, n_pool, PAGE_SIZE, H_C, HEAD_DIM), 1.0)
    cache_sb_v = nrm(ks[6], (N_ODD, n_pool, PAGE_SIZE, H_C, HEAD_DIM), 1.0)
    state_conv = nrm(ks[7], (N_ODD, DEC_BATCH, CONV_W - 1, D_D), 1.0)
    page_table = jax.random.permutation(ks[8], n_pool)[:n_used].reshape(DEC_BATCH, n_pages).astype(jnp.int32)

    g_mix = 1.0 + nrm(ks[9], (DEPTH, D_MODEL), 0.02)
    g_mlp = 1.0 + nrm(ks[10], (DEPTH, D_MODEL), 0.02)
    g_final = 1.0 + nrm(ks[11], (D_MODEL,), 0.02)
    w_up = nrm(ks[12], (DEPTH, D_MODEL, D_FF), D_MODEL ** -0.5)
    w_down = nrm(ks[13], (DEPTH, D_FF, D_MODEL), D_FF ** -0.5)
    w_in_even = nrm(ks[14], (N_EVEN, D_MODEL, sum(EVEN_WIDTHS)), D_MODEL ** -0.5)
    b_forget = FORGET_BIAS + nrm(ks[15], (N_EVEN, H_A), 0.1)
    sgu_g = 1.0 + nrm(ks[16], (N_EVEN, D_B), 0.02)
    sgu_b = nrm(ks[17], (N_EVEN, D_B), 0.02)
    w_spatial = nrm(ks[18], (N_EVEN, G_B, CHUNK, CHUNK), CHUNK ** -0.5)
    b_spatial = 1.0 + nrm(ks[19], (N_EVEN, G_B, CHUNK), 0.02)
    w_out_even = nrm(ks[20], (N_EVEN, MIX_WIDTH, D_MODEL), MIX_WIDTH ** -0.5)
    w_in_odd = nrm(ks[21], (N_ODD, D_MODEL, sum(ODD_WIDTHS)), D_MODEL ** -0.5)
    conv_w = nrm(ks[22], (N_ODD, CONV_W, D_D), CONV_W ** -0.5)
    w_out_odd = nrm(ks[23], (N_ODD, MIX_WIDTH, D_MODEL), MIX_WIDTH ** -0.5)
    return {
        "x_prompt": x_prompt, "x_sample": x_sample,
        "cache_fox_k": cache_fox_k, "cache_fox_v": cache_fox_v, "cache_fox_logf": cache_fox_logf,
        "cache_sb_k": cache_sb_k, "cache_sb_v": cache_sb_v, "state_conv": state_conv,
        "page_table": page_table,
        "g_mix": g_mix, "g_mlp": g_mlp, "g_final": g_final, "w_up": w_up, "w_down": w_down,
        "w_in_even": w_in_even, "b_forget": b_forget, "sgu_g": sgu_g, "sgu_b": sgu_b,
        "w_spatial": w_spatial, "b_spatial": b_spatial, "w_out_even": w_out_even,
        "w_in_odd": w_in_odd, "conv_w": conv_w, "w_out_odd": w_out_odd,
    }


def reference(x_prompt, x_sample, cache_fox_k, cache_fox_v, cache_fox_logf, cache_sb_k, cache_sb_v, state_conv,
              page_table, g_mix, g_mlp, g_final, w_up, w_down, w_in_even, b_forget, sgu_g, sgu_b, w_spatial,
              b_spatial, w_out_even, w_in_odd, conv_w, w_out_odd):
    y_prompt, p_even, p_odd = trunk(x_prompt, g_mix, g_mlp, g_final, w_up, w_down, w_in_even, b_forget, sgu_g,
                                    sgu_b, w_spatial, b_spatial, w_out_even, w_in_odd, conv_w, w_out_odd, None)
    cache = (page_table, cache_fox_k, cache_fox_v, cache_fox_logf, cache_sb_k, cache_sb_v, state_conv)
    y_sample, s_even, s_odd = trunk(x_sample, g_mix, g_mlp, g_final, w_up, w_down, w_in_even, b_forget, sgu_g,
                                    sgu_b, w_spatial, b_spatial, w_out_even, w_in_odd, conv_w, w_out_odd, cache)
    p_fox_k, p_fox_v, p_fox_logf, _ = p_even
    p_sb_k, p_sb_v, p_conv = p_odd
    s_fox_k, s_fox_v, s_fox_logf, s_sgu_v = s_even
    s_sb_k, s_sb_v, s_conv = s_odd
    return (y_prompt, y_sample, p_fox_k, p_fox_v, p_fox_logf, p_sb_k, p_sb_v, p_conv,
            s_fox_k, s_fox_v, s_fox_logf, s_sb_k, s_sb_v, s_conv, s_sgu_v)
```

```python
import functools

import numpy as np
import jax
import jax.numpy as jnp
from jax import lax
from jax.experimental import pallas as pl
from jax.experimental.pallas import tpu as pltpu

F32 = jnp.float32
BF16 = jnp.bfloat16

HEAD_DIM = 64
N_HEADS = 8
PAIR = 2 * HEAD_DIM
N_PAIRS = N_HEADS // 2
D_HALF = N_HEADS * HEAD_DIM
CHUNK = 128
PAGE = 128
CONV_W = 3
EPS = 1e-6
QK_SCALE = HEAD_DIM ** -0.5
NEG_BIG = -1e30
LANES = 128
VMEM_LIMIT = 56 * 1024 * 1024
PAGES_PER_STEP = 16

NT_DIMS = (((1,), (1,)), ((), ()))


def _dot(a, b):
    return jnp.dot(a, b, preferred_element_type=F32)


def _dot_nt(a, b):
    return lax.dot_general(a, b, NT_DIMS, preferred_element_type=F32)


def _dot_f32(a, b):
    return jnp.dot(a, b, preferred_element_type=F32, precision=lax.Precision.HIGHEST)


def _rms(x, g):
    return x * lax.rsqrt(jnp.mean(x * x, axis=-1, keepdims=True) + EPS) * g


def _log_sigmoid(x):
    return jnp.minimum(x, 0.0) - jnp.log1p(jnp.exp(-jnp.abs(x)))


def _params(*sem):
    return pltpu.CompilerParams(dimension_semantics=sem, vmem_limit_bytes=VMEM_LIMIT)


def _resident(shape):
    nd = len(shape)
    return pl.BlockSpec(shape, lambda *_: (0,) * nd, pipeline_mode=pl.Buffered(1))


def _row_tile(m, cap):
    t = min(m, cap)
    assert m % t == 0
    return t


def _tri(n, kind):
    r = lax.broadcasted_iota(jnp.int32, (n, n), 0)
    c = lax.broadcasted_iota(jnp.int32, (n, n), 1)
    keep = {"row_le_col": r <= c, "row_ge_col": r >= c, "row_gt_col": r > c}[kind]
    return jnp.where(keep, 1.0, 0.0).astype(F32)


def _emit_kv(hb, w_ref, full_ref, half_ref, kv_t):
    kv = _dot_nt(w_ref[...], hb) if kv_t else _dot(hb, w_ref[...])
    if kv_t:
        full_ref[0] = kv
        half_ref[0] = kv.astype(BF16)
    else:
        full_ref[...] = kv
        half_ref[...] = kv.astype(BF16)


def _proj_even_kernel(x_ref, g_ref, wq_ref, wk_ref, wv_ref, wf_ref, wft_ref, wu_ref, wg_ref,
                      bfc_ref, bfr_ref, sg_ref, sb_ref,
                      q_ref, k_ref, kb_ref, v_ref, vb_ref, lfc_ref, lfr_ref, u_ref, vn_ref, vnb_ref, *, kv_t):
    hb = _rms(x_ref[...], g_ref[...]).astype(BF16)
    q_ref[...] = (_dot(hb, wq_ref[...]) * QK_SCALE).astype(BF16)
    _emit_kv(hb, wk_ref, k_ref, kb_ref, kv_t)
    _emit_kv(hb, wv_ref, v_ref, vb_ref, kv_t)
    lfc_ref[...] = _log_sigmoid(_dot(hb, wf_ref[...]) + bfc_ref[...])
    lfr = _log_sigmoid(_dot_nt(wft_ref[...], hb)[:N_HEADS] + bfr_ref[...])
    if kv_t:
        lfr_ref[0] = lfr
    else:
        lfr_ref[...] = lfr
    u_ref[...] = _dot(hb, wu_ref[...])
    vg = _dot(hb, wg_ref[...])
    mu = jnp.mean(vg, axis=-1, keepdims=True)
    vc = vg - mu
    var = jnp.mean(vc * vc, axis=-1, keepdims=True)
    vn = vc * lax.rsqrt(var + EPS) * sg_ref[...] + sb_ref[...]
    vn_ref[...] = vn
    vnb_ref[...] = vn.astype(BF16)


def _tok_out(m, tm, w, dt):
    return jax.ShapeDtypeStruct((m, w), dt), pl.BlockSpec((tm, w), lambda i: (i, 0))


def _feat_out(b, t, tm, w, dt):
    nt = t // tm
    return jax.ShapeDtypeStruct((b, w, t), dt), pl.BlockSpec((1, w, tm), lambda i: (i // nt, 0, i % nt))


def _kv_outs(b, t, tm, kv_t):
    if kv_t:
        return [_feat_out(b, t, tm, D_HALF, F32), _feat_out(b, t, tm, D_HALF, BF16)]
    return [_tok_out(b * t, tm, D_HALF, F32), _tok_out(b * t, tm, D_HALF, BF16)]


def _proj_even(x, g, w_in, b_forget, sgu_g, sgu_b, b, t, kv_t):
    m, d = x.shape
    tm = _row_tile(t if kv_t else m, 512)
    wb = w_in.astype(BF16)
    o = 3 * D_HALF
    wq, wk, wv = wb[:, :D_HALF], wb[:, D_HALF:2 * D_HALF], wb[:, 2 * D_HALF:o]
    if kv_t:
        wk, wv = wk.T, wv.T
    wf = jnp.pad(wb[:, o:o + N_HEADS], ((0, 0), (0, LANES - N_HEADS)))
    wft = jnp.pad(wb[:, o:o + N_HEADS].T, ((0, 16 - N_HEADS), (0, 0)))
    wu, wg = wb[:, o + N_HEADS:o + N_HEADS + D_HALF], wb[:, o + N_HEADS + D_HALF:]
    bfc = jnp.pad(b_forget.astype(F32), (0, LANES - N_HEADS)).reshape(1, LANES)
    bfr = b_forget.astype(F32).reshape(N_HEADS, 1)
    lfr_out = (_feat_out(b, t, tm, N_HEADS, F32) if kv_t else
               (jax.ShapeDtypeStruct((N_HEADS, m), F32), pl.BlockSpec((N_HEADS, tm), lambda i: (0, i))))
    outs = ([_tok_out(m, tm, D_HALF, BF16)] + _kv_outs(b, t, tm, kv_t) + _kv_outs(b, t, tm, kv_t)
            + [_tok_out(m, tm, LANES, F32), lfr_out,
               _tok_out(m, tm, D_HALF, F32), _tok_out(m, tm, D_HALF, F32), _tok_out(m, tm, D_HALF, BF16)])
    weights = [wq, wk, wv, wf, wft, wu, wg, bfc, bfr,
               sgu_g.astype(F32).reshape(1, D_HALF), sgu_b.astype(F32).reshape(1, D_HALF)]
    return pl.pallas_call(
        functools.partial(_proj_even_kernel, kv_t=kv_t),
        grid=(m // tm,),
        in_specs=[pl.BlockSpec((tm, d), lambda i: (i, 0)), _resident((1, d))] + [_resident(w.shape) for w in weights],
        out_specs=[s for _, s in outs],
        out_shape=[s for s, _ in outs],
        compiler_params=_params("parallel"),
        name="proj_even",
    )(x, g.astype(F32).reshape(1, d), *weights)


def _proj_odd_kernel(x_ref, g_ref, wq_ref, wk_ref, wv_ref, wgb_ref, wgc_ref, wh_ref,
                     q_ref, k_ref, kb_ref, v_ref, vb_ref, gb_ref, u_ref, *, kv_t):
    hb = _rms(x_ref[...], g_ref[...]).astype(BF16)
    q_ref[...] = (_dot(hb, wq_ref[...]) * QK_SCALE).astype(BF16)
    _emit_kv(hb, wk_ref, k_ref, kb_ref, kv_t)
    _emit_kv(hb, wv_ref, v_ref, vb_ref, kv_t)
    gb_ref[...] = _dot(hb, wgb_ref[...])
    u_ref[...] = _dot(hb, wgc_ref[...]) * _dot(hb, wh_ref[...])


def _proj_odd(x, g, w_in, b, t, kv_t):
    m, d = x.shape
    tm = _row_tile(t if kv_t else m, 512)
    wb = w_in.astype(BF16)
    weights = [wb[:, i * D_HALF:(i + 1) * D_HALF] for i in range(6)]
    if kv_t:
        weights[1], weights[2] = weights[1].T, weights[2].T
    outs = ([_tok_out(m, tm, D_HALF, BF16)] + _kv_outs(b, t, tm, kv_t) + _kv_outs(b, t, tm, kv_t)
            + [_tok_out(m, tm, D_HALF, F32), _tok_out(m, tm, D_HALF, F32)])
    return pl.pallas_call(
        functools.partial(_proj_odd_kernel, kv_t=kv_t),
        grid=(m // tm,),
        in_specs=[pl.BlockSpec((tm, d), lambda i: (i, 0)), _resident((1, d))] + [_resident(w.shape) for w in weights],
        out_specs=[s for _, s in outs],
        out_shape=[s for s, _ in outs],
        compiler_params=_params("parallel"),
        name="proj_odd",
    )(x, g.astype(F32).reshape(1, d), *weights)


def _mix_mlp_kernel(x_ref, a_ref, b_ref, woa_ref, wob_ref, g_ref, wup_ref, wdn_ref, gf_ref, o_ref,
                    *, final_norm, ff_chunk):
    x1 = x_ref[...] + _dot(a_ref[...], woa_ref[...]) + _dot(b_ref[...], wob_ref[...])
    hb = _rms(x1, g_ref[...]).astype(BF16)
    y = None
    for c in range(wup_ref.shape[1] // ff_chunk):
        sl = slice(c * ff_chunk, (c + 1) * ff_chunk)
        up = jnp.maximum(_dot(hb, wup_ref[:, sl]), 0.0)
        down = _dot((up * up).astype(BF16), wdn_ref[sl, :])
        y = down if y is None else y + down
    out = x1 + y
    o_ref[...] = _rms(out, gf_ref[...]) if final_norm else out


def _mix_mlp(x, a, b, w_out, g_mlp, w_up, w_down, g_final, final_norm):
    m, d = x.shape
    tm = _row_tile(m, 512)
    wo = w_out.astype(BF16)
    weights = [wo[:D_HALF], wo[D_HALF:], g_mlp.astype(F32).reshape(1, d), w_up.astype(BF16),
               w_down.astype(BF16), g_final.astype(F32).reshape(1, d)]
    row = lambda i: (i, 0)
    return pl.pallas_call(
        functools.partial(_mix_mlp_kernel, final_norm=final_norm, ff_chunk=1024),
        grid=(m // tm,),
        in_specs=[pl.BlockSpec((tm, d), row), pl.BlockSpec((tm, D_HALF), row), pl.BlockSpec((tm, D_HALF), row)]
        + [_resident(w.shape) for w in weights],
        out_specs=pl.BlockSpec((tm, d), row),
        out_shape=jax.ShapeDtypeStruct((m, d), F32),
        compiler_params=_params("parallel"),
        name="mix_mlp",
    )(x, a, b, *weights)


def _sgu_kernel(vn_ref, u_ref, w_ref, mask_ref, bias_ref, o_ref):
    lo = lax.broadcasted_iota(jnp.int32, (1, PAIR), 1) < HEAD_DIM
    keep = mask_ref[...] > 0.0
    for gp in range(N_PAIRS):
        we = jnp.where(keep, w_ref[2 * gp], 0.0).astype(BF16)
        wo = jnp.where(keep, w_ref[2 * gp + 1], 0.0).astype(BF16)
        cols = slice(gp * PAIR, (gp + 1) * PAIR)
        for c in range(vn_ref.shape[0] // CHUNK):
            rows = slice(c * CHUNK, (c + 1) * CHUNK)
            vn2 = vn_ref[rows, cols]
            mixed = jnp.where(lo, _dot(we, vn2), _dot(wo, vn2)) + bias_ref[:, cols]
            o_ref[rows, cols] = (u_ref[rows, cols] * mixed).astype(BF16)


def _sgu(vnb, u, w, mask, bias):
    m = vnb.shape[0]
    assert m % CHUNK == 0
    tm = _row_tile(m, 512)
    row = lambda i: (i, 0)
    return pl.pallas_call(
        _sgu_kernel,
        grid=(m // tm,),
        in_specs=[pl.BlockSpec((tm, D_HALF), row), pl.BlockSpec((tm, D_HALF), row),
                  _resident(w.shape), _resident(mask.shape), _resident(bias.shape)],
        out_specs=pl.BlockSpec((tm, D_HALF), row),
        out_shape=jax.ShapeDtypeStruct((m, D_HALF), BF16),
        compiler_params=_params("parallel"),
        name="sgu",
    )(vnb, u, w, mask, bias)


def _conv_kernel(u_ref, gb_ref, prev_ref, w_ref, o_ref, new_ref, carry_ref):
    i = pl.program_id(1)

    @pl.when(i == 0)
    def _():
        carry_ref[...] = jnp.zeros_like(carry_ref)
        carry_ref[6:8, :] = prev_ref[0]

    u = u_ref[0]
    tt = u.shape[0]
    prev = carry_ref[...]
    r = lax.broadcasted_iota(jnp.int32, (tt, 1), 0)
    u1 = pltpu.roll(u, 1, 0)
    u2 = pltpu.roll(u, 2, 0)
    p1 = jnp.broadcast_to(prev[7:8], u.shape)
    p2 = jnp.where(r == 0, jnp.broadcast_to(prev[6:7], u.shape), p1)
    u1 = jnp.where(r == 0, p1, u1)
    u2 = jnp.where(r <= 1, p2, u2)
    conv = w_ref[0:1] * u2 + w_ref[1:2] * u1 + w_ref[2:3] * u
    o_ref[0] = (gb_ref[0] * conv).astype(BF16)
    if tt >= 8:
        tail = u[tt - 8:tt]
    else:
        tail = jnp.concatenate([prev[tt:8], u], axis=0)
    carry_ref[...] = tail
    new_ref[0] = tail[6:8]


def _conv(u, gb, prev, conv_w):
    b, t, d = u.shape
    tt = _row_tile(t, 1024)
    blk = lambda bi, i: (bi, i, 0)
    return pl.pallas_call(
        _conv_kernel,
        grid=(b, t // tt),
        in_specs=[pl.BlockSpec((1, tt, d), blk), pl.BlockSpec((1, tt, d), blk),
                  pl.BlockSpec((1, CONV_W - 1, d), lambda bi, i: (bi, 0, 0)), _resident((CONV_W, d))],
        out_specs=[pl.BlockSpec((1, tt, d), blk), pl.BlockSpec((1, CONV_W - 1, d), lambda bi, i: (bi, 0, 0))],
        out_shape=[jax.ShapeDtypeStruct((b, t, d), BF16), jax.ShapeDtypeStruct((b, CONV_W - 1, d), F32)],
        scratch_shapes=[pltpu.VMEM((8, d), F32)],
        compiler_params=_params("parallel", "arbitrary"),
        name="short_conv",
    )(u, gb, prev.astype(F32), conv_w.astype(F32))


def _cumf_kernel(lfc_ref, lfr_ref, fc_ref, fr_ref):
    t = lfc_ref.shape[0]
    lower = _tri(CHUNK, "row_ge_col")
    upper = _tri(CHUNK, "row_le_col")

    def body(c, carry):
        cc, cr = carry
        o = pl.multiple_of(c * CHUNK, CHUNK)
        fc = _dot_f32(lower, lfc_ref[pl.ds(o, CHUNK), :]) + cc
        fr = _dot_f32(lfr_ref[0, :, pl.ds(o, CHUNK)], upper) + cr
        fc_ref[pl.ds(o, CHUNK), :] = fc
        fr_ref[0, :, pl.ds(o, CHUNK)] = fr
        return fc[CHUNK - 1:CHUNK, :], fr[:, CHUNK - 1:CHUNK]

    lax.fori_loop(0, t // CHUNK, body, (jnp.zeros((1, LANES), F32), jnp.zeros((N_HEADS, 1), F32)))


def _cumf(lfc, lfr, b, t):
    col = pl.BlockSpec((t, LANES), lambda i: (i, 0))
    rowm = pl.BlockSpec((1, N_HEADS, t), lambda i: (i, 0, 0))
    return pl.pallas_call(
        _cumf_kernel,
        grid=(b,),
        in_specs=[col, rowm],
        out_specs=[col, rowm],
        out_shape=[jax.ShapeDtypeStruct((b * t, LANES), F32), jax.ShapeDtypeStruct((b, N_HEADS, t), F32)],
        compiler_params=_params("parallel"),
        name="cum_logf",
    )(lfc, lfr)


def _stack_pair(q2):
    lo = lax.broadcasted_iota(jnp.int32, (1, PAIR), 1) < HEAD_DIM
    zero = jnp.zeros_like(q2)
    return jnp.concatenate([jnp.where(lo, q2, zero), jnp.where(lo, zero, q2)], axis=0)


def _unstack_pair(acc, tq):
    lo = lax.broadcasted_iota(jnp.int32, (1, PAIR), 1) < HEAD_DIM
    return jnp.where(lo, acc[:tq], acc[tq:])


def _fox_kernel(q_ref, kt_ref, vt_ref, fq_ref, fk_ref, o_ref, *, tq, tk):
    hp = pl.program_id(1)
    i = pl.program_id(2)
    qs = _stack_pair(q_ref[...])
    lane = lax.broadcasted_iota(jnp.int32, (1, LANES), 1)
    fq = fq_ref[...]
    fq_e = jnp.sum(jnp.where(lane == 2 * hp, fq, 0.0), axis=1, keepdims=True)
    fq_o = jnp.sum(jnp.where(lane == 2 * hp + 1, fq, 0.0), axis=1, keepdims=True)
    fqs = jnp.concatenate([fq_e, fq_o], axis=0)
    qpos = lax.broadcasted_iota(jnp.int32, (tq, tk), 0)
    kidx = lax.broadcasted_iota(jnp.int32, (tq, tk), 1)

    def block(j, carry, diag_off):
        m, l, acc = carry
        o = pl.multiple_of(j * tk, tk)
        s = _dot(qs, kt_ref[0, :, pl.ds(o, tk)])
        fk = fk_ref[0, 0, :, pl.ds(o, tk)]
        fkb = jnp.concatenate([jnp.broadcast_to(fk[0:1], (tq, tk)), jnp.broadcast_to(fk[1:2], (tq, tk))], axis=0)
        s = s + (fqs - fkb)
        if diag_off is not None:
            ok = kidx + diag_off <= qpos
            s = jnp.where(jnp.concatenate([ok, ok], axis=0), s, NEG_BIG)
        m_new = jnp.maximum(m, jnp.max(s, axis=1, keepdims=True))
        alpha = jnp.exp(m - m_new)
        p = jnp.exp(s - m_new)
        l = alpha * l + jnp.sum(p, axis=1, keepdims=True)
        acc = alpha * acc + _dot_nt(p.astype(BF16), vt_ref[0, :, pl.ds(o, tk)])
        return m_new, l, acc

    init = (jnp.full((2 * tq, 1), NEG_BIG, F32), jnp.zeros((2 * tq, 1), F32), jnp.zeros((2 * tq, PAIR), F32))
    per = tq // tk
    carry = lax.fori_loop(0, i * per, functools.partial(block, diag_off=None), init)
    for d in range(per):
        carry = block(i * per + d, carry, d * tk)
    _, l, acc = carry
    o_ref[...] = _unstack_pair(acc / l, tq).astype(BF16)


def _attn_specs(t, tq, nq):
    q_spec = pl.BlockSpec((tq, PAIR), lambda bi, hp, i: (bi * nq + i, hp))
    kv_spec = pl.BlockSpec((1, PAIR, t), lambda bi, hp, i: (bi, hp, 0))
    return q_spec, kv_spec


def _fox_attn(qb, ktb, vtb, fcol, frow, b, t):
    tq = tk = min(t, 256)
    nq = t // tq
    q_spec, kv_spec = _attn_specs(t, tq, nq)
    return pl.pallas_call(
        functools.partial(_fox_kernel, tq=tq, tk=tk),
        grid=(b, N_PAIRS, nq),
        in_specs=[q_spec, kv_spec, kv_spec,
                  pl.BlockSpec((tq, LANES), lambda bi, hp, i: (bi * nq + i, 0)),
                  pl.BlockSpec((1, 1, 2, t), lambda bi, hp, i: (bi, hp, 0, 0))],
        out_specs=q_spec,
        out_shape=jax.ShapeDtypeStruct((b * t, D_HALF), BF16),
        compiler_params=_params("parallel", "parallel", "arbitrary"),
        name="fox_attn",
    )(qb, ktb, vtb, fcol, frow.reshape(b, N_PAIRS, 2, t))


def _sb_block(qs, kt, vt, run, acc, strict_tri, mask):
    z = _dot(qs, kt)
    lk = jnp.minimum(-z, 0.0) - jnp.log1p(jnp.exp(-jnp.abs(z)))
    if mask is not None:
        lk = jnp.where(mask, lk, 0.0)
    hi = lk.astype(BF16)
    lo = (lk - hi.astype(F32)).astype(BF16)
    later = _dot(hi, strict_tri) + _dot(lo, strict_tri) + run
    a = jnp.exp(z + lk + later)
    if mask is not None:
        a = jnp.where(mask, a, 0.0)
    acc = acc + _dot_nt(a.astype(BF16), vt)
    run = run + jnp.sum(lk, axis=1, keepdims=True)
    return run, acc


def _sb_kernel(q_ref, kt_ref, vt_ref, o_ref, *, tq, tk):
    i = pl.program_id(2)
    qs = _stack_pair(q_ref[...])
    tri = _tri(tk, "row_gt_col").astype(BF16)
    qpos = lax.broadcasted_iota(jnp.int32, (tq, tk), 0)
    kidx = lax.broadcasted_iota(jnp.int32, (tq, tk), 1)
    per = tq // tk
    carry = (jnp.zeros((2 * tq, 1), F32), jnp.zeros((2 * tq, PAIR), F32))
    for d in reversed(range(per)):
        ok = kidx + d * tk < qpos
        o = pl.multiple_of((i * per + d) * tk, tk)
        carry = _sb_block(qs, kt_ref[0, :, pl.ds(o, tk)], vt_ref[0, :, pl.ds(o, tk)], *carry, tri,
                          jnp.concatenate([ok, ok], axis=0))

    def body(n, c):
        o = pl.multiple_of((i * per - 1 - n) * tk, tk)
        return _sb_block(qs, kt_ref[0, :, pl.ds(o, tk)], vt_ref[0, :, pl.ds(o, tk)], *c, tri, None)

    _, acc = lax.fori_loop(0, i * per, body, carry)
    o_ref[...] = _unstack_pair(acc, tq).astype(BF16)


def _sb_attn(qb, ktb, vtb, b, t):
    tq = tk = min(t, 256)
    nq = t // tq
    q_spec, kv_spec = _attn_specs(t, tq, nq)
    return pl.pallas_call(
        functools.partial(_sb_kernel, tq=tq, tk=tk),
        grid=(b, N_PAIRS, nq),
        in_specs=[q_spec, kv_spec, kv_spec],
        out_specs=q_spec,
        out_shape=jax.ShapeDtypeStruct((b * t, D_HALF), BF16),
        compiler_params=_params("parallel", "parallel", "arbitrary"),
        name="sb_attn",
    )(qb, ktb, vtb)


def _page_specs(block, n):
    nd = len(block) - 1
    return [pl.BlockSpec(block, functools.partial(lambda b, g, pt, p: (pt[b, g * n + p],) + (0,) * nd, p=p))
            for p in range(n)]


def _page_specs_rev(block, n, groups):
    nd = len(block) - 1

    def imap(b, g, pt, p):
        grp = groups - jnp.maximum(g, 1)
        return (pt[b, grp * n + p],) + (0,) * nd

    return [pl.BlockSpec(block, functools.partial(imap, p=p)) for p in range(n)]


def _cumf_dec_kernel(pt_ref, *refs):
    n = PAGES_PER_STEP
    pages, lfn_ref, fp_ref, fn_ref, carry_ref = refs[:n], refs[n], refs[n + 1], refs[n + 2], refs[n + 3]
    g = pl.program_id(1)
    upper = _tri(PAGE, "row_le_col")

    @pl.when(g == 0)
    def _():
        carry_ref[...] = jnp.zeros_like(carry_ref)

    carry = carry_ref[:, 0:1]
    for p in range(n):
        f = _dot_f32(pages[p][0], upper) + carry
        fp_ref[0, :, p * PAGE:(p + 1) * PAGE] = f
        carry = f[:, PAGE - 1:PAGE]
    carry_ref[...] = jnp.broadcast_to(carry, carry_ref.shape)
    fn_ref[0] = _dot_f32(lfn_ref[0], upper) + carry


def _cumf_dec(page_table, lf_pool_t, lf_new):
    nb, n_pages = page_table.shape
    n = PAGES_PER_STEP
    groups = n_pages // n
    return pl.pallas_call(
        _cumf_dec_kernel,
        grid_spec=pltpu.PrefetchScalarGridSpec(
            num_scalar_prefetch=1,
            grid=(nb, groups),
            in_specs=_page_specs((1, N_HEADS, PAGE), n) + [pl.BlockSpec((1, N_HEADS, PAGE), lambda b, g, pt: (b, 0, 0))],
            out_specs=[pl.BlockSpec((1, N_HEADS, n * PAGE), lambda b, g, pt: (b, 0, g)),
                       pl.BlockSpec((1, N_HEADS, PAGE), lambda b, g, pt: (b, 0, 0))],
            scratch_shapes=[pltpu.VMEM((N_HEADS, LANES), F32)]),
        out_shape=[jax.ShapeDtypeStruct((nb, N_HEADS, n_pages * PAGE), F32),
                   jax.ShapeDtypeStruct((nb, N_HEADS, PAGE), F32)],
        compiler_params=_params("parallel", "arbitrary"),
        name="cum_logf_paged",
    )(page_table, *([lf_pool_t] * n), lf_new)


def _block_diag_q(q):
    t = q.shape[0]
    rep = jnp.concatenate([q.astype(F32)] * N_HEADS, axis=0)
    rh = lax.broadcasted_iota(jnp.int32, (N_HEADS * t, D_HALF), 0) // t
    lh = lax.broadcasted_iota(jnp.int32, (N_HEADS * t, D_HALF), 1) // HEAD_DIM
    return jnp.where(rh == lh, rep, 0.0).astype(BF16)


def _collapse_heads(acc, t):
    rh = lax.broadcasted_iota(jnp.int32, (N_HEADS * t, D_HALF), 0) // t
    lh = lax.broadcasted_iota(jnp.int32, (N_HEADS * t, D_HALF), 1) // HEAD_DIM
    masked = jnp.where(rh == lh, acc, 0.0)
    out = masked[0:t]
    for h in range(1, N_HEADS):
        out = out + masked[h * t:(h + 1) * t]
    return out


def _rep_heads(x, t):
    return jnp.concatenate([jnp.broadcast_to(x[h:h + 1], (t, x.shape[1])) for h in range(N_HEADS)], axis=0)


def _fox_dec_kernel(pt_ref, *refs, nt):
    n = PAGES_PER_STEP
    kp, vp = refs[:n], refs[n:2 * n]
    q_ref, kn_ref, vn_ref, fp_ref, fn_ref, o_ref, ks_ref, vs_ref, m_ref, l_ref, acc_ref = refs[2 * n:]
    g = pl.program_id(1)
    rows = N_HEADS * nt

    @pl.when(g == 0)
    def _():
        m_ref[...] = jnp.full_like(m_ref, NEG_BIG)
        l_ref[...] = jnp.zeros_like(l_ref)
        acc_ref[...] = jnp.zeros_like(acc_ref)

    for p in range(n):
        ks_ref[:, p * PAGE:(p + 1) * PAGE] = kp[p][0].astype(BF16)
        vs_ref[:, p * PAGE:(p + 1) * PAGE] = vp[p][0].astype(BF16)

    qbd = _block_diag_q(q_ref[0])
    fnew = _rep_heads(fn_ref[0], nt)
    tpos = lax.broadcasted_iota(jnp.int32, (rows, LANES), 0) % nt
    lane = lax.broadcasted_iota(jnp.int32, (rows, LANES), 1)
    fq = jnp.sum(jnp.where(lane == tpos, fnew, 0.0), axis=1, keepdims=True)

    def update(s, vt):
        m_old = m_ref[:, 0:1]
        m_new = jnp.maximum(m_old, jnp.max(s, axis=1, keepdims=True))
        alpha = jnp.exp(m_old - m_new)
        p_ = jnp.exp(s - m_new)
        l_ref[...] = jnp.broadcast_to(alpha * l_ref[:, 0:1] + jnp.sum(p_, axis=1, keepdims=True), l_ref.shape)
        acc_ref[...] = alpha * acc_ref[...] + _dot_nt(p_.astype(BF16), vt)
        m_ref[...] = jnp.broadcast_to(m_new, m_ref.shape)

    s = _dot(qbd, ks_ref[...]) + (fq - _rep_heads(fp_ref[0], nt))
    update(s, vs_ref[...])

    @pl.when(g == pl.num_programs(1) - 1)
    def _():
        sn = _dot(qbd, kn_ref[0]) + (fq - fnew)
        sn = jnp.where(lane <= tpos, sn, NEG_BIG)
        update(sn, vn_ref[0])
        o_ref[0] = _collapse_heads(acc_ref[...] / l_ref[:, 0:1], nt).astype(BF16)


def _fox_dec(page_table, k_pool, v_pool, qb, knt, vnt, f_past, f_new):
    nb, n_pages = page_table.shape
    nt = qb.shape[1]
    n = PAGES_PER_STEP
    groups = n_pages // n
    rows = N_HEADS * nt
    seq = lambda b, g, pt: (b, 0, 0)
    return pl.pallas_call(
        functools.partial(_fox_dec_kernel, nt=nt),
        grid_spec=pltpu.PrefetchScalarGridSpec(
            num_scalar_prefetch=1,
            grid=(nb, groups),
            in_specs=_page_specs((1, D_HALF, PAGE), n) + _page_specs((1, D_HALF, PAGE), n)
            + [pl.BlockSpec((1, nt, D_HALF), seq), pl.BlockSpec((1, D_HALF, PAGE), seq),
               pl.BlockSpec((1, D_HALF, PAGE), seq),
               pl.BlockSpec((1, N_HEADS, n * PAGE), lambda b, g, pt: (b, 0, g)),
               pl.BlockSpec((1, N_HEADS, PAGE), seq)],
            out_specs=pl.BlockSpec((1, nt, D_HALF), seq),
            scratch_shapes=[pltpu.VMEM((D_HALF, n * PAGE), BF16), pltpu.VMEM((D_HALF, n * PAGE), BF16),
                            pltpu.VMEM((rows, LANES), F32), pltpu.VMEM((rows, LANES), F32),
                            pltpu.VMEM((rows, D_HALF), F32)]),
        out_shape=jax.ShapeDtypeStruct((nb, nt, D_HALF), BF16),
        compiler_params=_params("parallel", "arbitrary"),
        name="fox_decode",
    )(page_table, *([k_pool] * n), *([v_pool] * n), qb, knt, vnt, f_past, f_new)


def _sb_dec_kernel(pt_ref, *refs, nt):
    n = PAGES_PER_STEP
    kp, vp = refs[:n], refs[n:2 * n]
    q_ref, kn_ref, vn_ref, o_ref, run_ref, acc_ref = refs[2 * n:]
    g = pl.program_id(1)
    rows = N_HEADS * nt
    tri = _tri(PAGE, "row_gt_col").astype(BF16)
    qbd = _block_diag_q(q_ref[0])

    @pl.when(g == 0)
    def _():
        tpos = lax.broadcasted_iota(jnp.int32, (rows, LANES), 0) % nt
        lane = lax.broadcasted_iota(jnp.int32, (rows, LANES), 1)
        run, acc = _sb_block(qbd, kn_ref[0], vn_ref[0], jnp.zeros((rows, 1), F32),
                             jnp.zeros((rows, D_HALF), F32), tri, lane < tpos)
        run_ref[...] = jnp.broadcast_to(run, run_ref.shape)
        acc_ref[...] = acc

    @pl.when(g > 0)
    def _():
        run = run_ref[:, 0:1]
        acc = acc_ref[...]
        for p in reversed(range(n)):
            run, acc = _sb_block(qbd, kp[p][0].astype(BF16), vp[p][0].astype(BF16), run, acc, tri, None)
        run_ref[...] = jnp.broadcast_to(run, run_ref.shape)
        acc_ref[...] = acc

    @pl.when(g == pl.num_programs(1) - 1)
    def _():
        o_ref[0] = _collapse_heads(acc_ref[...], nt).astype(BF16)


def _sb_dec(page_table, k_pool, v_pool, qb, knt, vnt):
    nb, n_pages = page_table.shape
    nt = qb.shape[1]
    n = PAGES_PER_STEP
    groups = n_pages // n
    rows = N_HEADS * nt
    seq = lambda b, g, pt: (b, 0, 0)
    return pl.pallas_call(
        functools.partial(_sb_dec_kernel, nt=nt),
        grid_spec=pltpu.PrefetchScalarGridSpec(
            num_scalar_prefetch=1,
            grid=(nb, groups + 1),
            in_specs=_page_specs_rev((1, D_HALF, PAGE), n, groups) + _page_specs_rev((1, D_HALF, PAGE), n, groups)
            + [pl.BlockSpec((1, nt, D_HALF), seq), pl.BlockSpec((1, D_HALF, PAGE), seq),
               pl.BlockSpec((1, D_HALF, PAGE), seq)],
            out_specs=pl.BlockSpec((1, nt, D_HALF), seq),
            scratch_shapes=[pltpu.VMEM((rows, LANES), F32), pltpu.VMEM((rows, D_HALF), F32)]),
        out_shape=jax.ShapeDtypeStruct((nb, nt, D_HALF), BF16),
        compiler_params=_params("parallel", "arbitrary"),
        name="sb_decode",
    )(page_table, *([k_pool] * n), *([v_pool] * n), qb, knt, vnt)


def _new_feat_major(kb, b, t):
    return jnp.pad(kb.reshape(b, t, D_HALF).transpose(0, 2, 1), ((0, 0), (0, 0), (0, PAGE - t)))


def _heads_last(kt, b, t):
    return kt.reshape(b, N_HEADS, HEAD_DIM, t).transpose(0, 3, 1, 2)[None]


def _trunk(x, p, cache):
    b, t, d = x.shape
    m = b * t
    x2 = x.reshape(m, d)
    prompt = cache is None

    qb, k, kb, v, vb, lfc, lfr, u, vn, vnb = _proj_even(x2, p["g_mix"][0], p["w_in_even"][0], p["b_forget"][0],
                                                        p["sgu_g"][0], p["sgu_b"][0], b, t, prompt)
    w_s, b_s = p["w_spatial"][0].astype(F32), p["b_spatial"][0].astype(F32)
    if prompt:
        fcol, frow = _cumf(lfc, lfr, b, t)
        a = _fox_attn(qb, kb, vb, fcol, frow, b, t)
        mask = np.tril(np.ones((CHUNK, CHUNK), np.float32))
        bias = jnp.repeat(b_s.T, HEAD_DIM, axis=1)
        w_mix = w_s
        even_rows = (_heads_last(k, b, t), _heads_last(v, b, t), lfr.transpose(0, 2, 1)[None], None)
    else:
        pt = cache["page_table"]
        lf_new = jnp.pad(lfr.reshape(N_HEADS, b, t).transpose(1, 0, 2), ((0, 0), (0, 0), (0, PAGE - t)))
        f_past, f_new = _cumf_dec(pt, cache["fox_logf_t"], lf_new)
        a = _fox_dec(pt, cache["fox_k"], cache["fox_v"], qb.reshape(b, t, D_HALF),
                     _new_feat_major(kb, b, t), _new_feat_major(vb, b, t), f_past, f_new).reshape(m, D_HALF)
        reps = CHUNK // t
        idx = np.arange(CHUNK)
        mask = ((idx[:, None] // t == idx[None, :] // t) & (idx[None, :] % t <= idx[:, None] % t)).astype(np.float32)
        w_mix = jnp.tile(w_s[:, :t, :t], (1, reps, reps))
        bias = jnp.tile(jnp.repeat(b_s.T[:t], HEAD_DIM, axis=1), (reps, 1))
        hd = lambda z: z.reshape(1, b, t, N_HEADS, HEAD_DIM)
        even_rows = (hd(k), hd(v), lfc[:, :N_HEADS].reshape(1, b, t, N_HEADS), vn.reshape(1, b, t, D_HALF))
    ob = _sgu(vnb, u, w_mix, jnp.asarray(mask), bias)
    x2 = _mix_mlp(x2, a, ob, p["w_out_even"][0], p["g_mlp"][0], p["w_up"][0], p["w_down"][0], p["g_final"], False)

    qb, k, kb, v, vb, gb, uc = _proj_odd(x2, p["g_mix"][1], p["w_in_odd"][0], b, t, prompt)
    if prompt:
        a = _sb_attn(qb, kb, vb, b, t)
        prev = jnp.zeros((b, CONV_W - 1, D_HALF), F32)
        kv_rows = (_heads_last(k, b, t), _heads_last(v, b, t))
    else:
        a = _sb_dec(cache["page_table"], cache["sb_k"], cache["sb_v"], qb.reshape(b, t, D_HALF),
                    _new_feat_major(kb, b, t), _new_feat_major(vb, b, t)).reshape(m, D_HALF)
        prev = cache["conv"]
        kv_rows = (k.reshape(1, b, t, N_HEADS, HEAD_DIM), v.reshape(1, b, t, N_HEADS, HEAD_DIM))
    od, new_conv = _conv(uc.reshape(b, t, D_HALF), gb.reshape(b, t, D_HALF), prev, p["conv_w"][0])
    y = _mix_mlp(x2, a, od.reshape(m, D_HALF), p["w_out_odd"][0], p["g_mlp"][1], p["w_up"][1], p["w_down"][1],
                 p["g_final"], True)
    return y.reshape(b, t, d), even_rows, kv_rows + (new_conv[None],)


def _pool_feat_major(cache):
    n_pool = cache.shape[1]
    return cache[0].transpose(0, 2, 3, 1).reshape(n_pool, D_HALF, PAGE)


def kernel(x_prompt, x_sample, cache_fox_k, cache_fox_v, cache_fox_logf, cache_sb_k, cache_sb_v, state_conv,
           page_table, g_mix, g_mlp, g_final, w_up, w_down, w_in_even, b_forget, sgu_g, sgu_b, w_spatial,
           b_spatial, w_out_even, w_in_odd, conv_w, w_out_odd):
    p = dict(g_mix=g_mix, g_mlp=g_mlp, g_final=g_final, w_up=w_up, w_down=w_down, w_in_even=w_in_even,
             b_forget=b_forget, sgu_g=sgu_g, sgu_b=sgu_b, w_spatial=w_spatial, b_spatial=b_spatial,
             w_out_even=w_out_even, w_in_odd=w_in_odd, conv_w=conv_w, w_out_odd=w_out_odd)
    y_p, (pk, pv, plf, _), (psk, psv, pconv) = _trunk(x_prompt, p, None)
    cache = dict(page_table=page_table,
                 fox_k=_pool_feat_major(cache_fox_k), fox_v=_pool_feat_major(cache_fox_v),
                 fox_logf_t=cache_fox_logf[0].transpose(0, 2, 1),
                 sb_k=_pool_feat_major(cache_sb_k), sb_v=_pool_feat_major(cache_sb_v),
                 conv=state_conv[0])
    y_s, (sk, sv, slf, svn), (ssk, ssv, sconv) = _trunk(x_sample, p, cache)
    return (y_p, y_s, pk, pv, plf, psk, psv, pconv, sk, sv, slf, ssk, ssv, sconv, svn)
```

```python
import functools

import numpy as np
import jax
import jax.numpy as jnp
from jax import lax
from jax.experimental import pallas as pl
from jax.experimental.pallas import tpu as pltpu

F32 = jnp.float32
BF16 = jnp.bfloat16

HEAD_DIM = 64
N_HEADS = 8
PAIR = 2 * HEAD_DIM
N_PAIRS = N_HEADS // 2
D_HALF = N_HEADS * HEAD_DIM
CHUNK = 128
PAGE = 128
CONV_W = 3
EPS = 1e-6
QK_SCALE = HEAD_DIM ** -0.5
NEG_BIG = -1e30
LANES = 128
VMEM_LIMIT = 56 * 1024 * 1024
PAGES_PER_STEP = 16
LOGF_PAGES_PER_STEP = 64
FOX_BLOCK = 512
SB_BLOCK = 256
SB_LOG_FLOOR = -110.0

NT_DIMS = (((1,), (1,)), ((), ()))


def _dot(a, b):
    return jnp.dot(a, b, preferred_element_type=F32)


def _dot_nt(a, b):
    return lax.dot_general(a, b, NT_DIMS, preferred_element_type=F32)


def _dot_f32(a, b):
    return jnp.dot(a, b, preferred_element_type=F32, precision=lax.Precision.HIGHEST)


def _rms(x, g):
    return x * lax.rsqrt(jnp.mean(x * x, axis=-1, keepdims=True) + EPS) * g


def _log_sigmoid(x):
    return jnp.minimum(x, 0.0) - jnp.log1p(jnp.exp(-jnp.abs(x)))


def _params(*sem):
    return pltpu.CompilerParams(dimension_semantics=sem, vmem_limit_bytes=VMEM_LIMIT)


def _resident(shape):
    nd = len(shape)
    return pl.BlockSpec(shape, lambda *_: (0,) * nd, pipeline_mode=pl.Buffered(1))


def _row_tile(m, cap):
    t = min(m, cap)
    assert m % t == 0
    return t


def _tri(n, kind):
    r = lax.broadcasted_iota(jnp.int32, (n, n), 0)
    c = lax.broadcasted_iota(jnp.int32, (n, n), 1)
    keep = {"row_le_col": r <= c, "row_ge_col": r >= c, "row_gt_col": r > c, "row_lt_col": r < c}[kind]
    return jnp.where(keep, 1.0, 0.0).astype(F32)


def _emit_qkv(hb, w_refs, o_refs, prompt):
    if prompt:
        wqt, wk, wkt, wvt = w_refs
        qt_ref, kb_ref, kt_ref, vt_ref, vtb_ref = o_refs
        qt_ref[0] = (_dot_nt(wqt[...], hb) * QK_SCALE).astype(BF16)
        kb_ref[...] = _dot(hb, wk[...]).astype(BF16)
        kt_ref[0] = _dot_nt(wkt[...], hb)
        vt = _dot_nt(wvt[...], hb)
        vt_ref[0] = vt
        vtb_ref[0] = vt.astype(BF16)
    else:
        wq, wk, wv = w_refs
        q_ref, k_ref, kb_ref, v_ref, vb_ref = o_refs
        q_ref[...] = (_dot(hb, wq[...]) * QK_SCALE).astype(BF16)
        k = _dot(hb, wk[...])
        k_ref[...] = k
        kb_ref[...] = k.astype(BF16)
        v = _dot(hb, wv[...])
        v_ref[...] = v
        vb_ref[...] = v.astype(BF16)


def _qkv_weights(wq, wk, wv, prompt):
    return [wq.T, wk, wk.T, wv.T] if prompt else [wq, wk, wv]


def _tok_out(m, tm, w, dt):
    return jax.ShapeDtypeStruct((m, w), dt), pl.BlockSpec((tm, w), lambda i: (i, 0))


def _feat_out(b, t, tm, w, dt):
    nt = t // tm
    return jax.ShapeDtypeStruct((b, w, t), dt), pl.BlockSpec((1, w, tm), lambda i: (i // nt, 0, i % nt))


def _qkv_outs(b, t, tm, prompt):
    m = b * t
    if prompt:
        return [_feat_out(b, t, tm, D_HALF, BF16), _tok_out(m, tm, D_HALF, BF16), _feat_out(b, t, tm, D_HALF, F32),
                _feat_out(b, t, tm, D_HALF, F32), _feat_out(b, t, tm, D_HALF, BF16)]
    return [_tok_out(m, tm, D_HALF, BF16), _tok_out(m, tm, D_HALF, F32), _tok_out(m, tm, D_HALF, BF16),
            _tok_out(m, tm, D_HALF, F32), _tok_out(m, tm, D_HALF, BF16)]


def _proj_even_kernel(*refs, prompt):
    nw = 4 if prompt else 3
    x_ref, g_ref = refs[:2]
    w_refs = refs[2:2 + nw]
    wf_ref, wft_ref, wu_ref, wg_ref, bfc_ref, bfr_ref, sg_ref, sb_ref = refs[2 + nw:10 + nw]
    o_refs = refs[10 + nw:15 + nw]
    lfc_ref, lfr_ref, u_ref, vn_ref, vnb_ref = refs[15 + nw:]
    hb = _rms(x_ref[...], g_ref[...]).astype(BF16)
    _emit_qkv(hb, w_refs, o_refs, prompt)
    lfc_ref[...] = _log_sigmoid(_dot(hb, wf_ref[...]) + bfc_ref[...])
    lfr = _log_sigmoid(_dot_nt(wft_ref[...], hb)[:N_HEADS] + bfr_ref[...])
    if prompt:
        lfr_ref[0] = lfr
    else:
        lfr_ref[...] = lfr
    u_ref[...] = _dot(hb, wu_ref[...])
    vg = _dot(hb, wg_ref[...])
    mu = jnp.mean(vg, axis=-1, keepdims=True)
    vc = vg - mu
    var = jnp.mean(vc * vc, axis=-1, keepdims=True)
    vn = vc * lax.rsqrt(var + EPS) * sg_ref[...] + sb_ref[...]
    vn_ref[...] = vn
    vnb_ref[...] = vn.astype(BF16)


def _proj_even(x, g, w_in, b_forget, sgu_g, sgu_b, b, t, prompt):
    m, d = x.shape
    tm = _row_tile(t if prompt else m, 512)
    wb = w_in.astype(BF16)
    o = 3 * D_HALF
    qkv_w = _qkv_weights(wb[:, :D_HALF], wb[:, D_HALF:2 * D_HALF], wb[:, 2 * D_HALF:o], prompt)
    wf = jnp.pad(wb[:, o:o + N_HEADS], ((0, 0), (0, LANES - N_HEADS)))
    wft = jnp.pad(wb[:, o:o + N_HEADS].T, ((0, 16 - N_HEADS), (0, 0)))
    wu, wg = wb[:, o + N_HEADS:o + N_HEADS + D_HALF], wb[:, o + N_HEADS + D_HALF:]
    bfc = jnp.pad(b_forget.astype(F32), (0, LANES - N_HEADS)).reshape(1, LANES)
    bfr = b_forget.astype(F32).reshape(N_HEADS, 1)
    lfr_out = (_feat_out(b, t, tm, N_HEADS, F32) if prompt else
               (jax.ShapeDtypeStruct((N_HEADS, m), F32), pl.BlockSpec((N_HEADS, tm), lambda i: (0, i))))
    outs = (_qkv_outs(b, t, tm, prompt)
            + [_tok_out(m, tm, LANES, F32), lfr_out,
               _tok_out(m, tm, D_HALF, F32), _tok_out(m, tm, D_HALF, F32), _tok_out(m, tm, D_HALF, BF16)])
    weights = qkv_w + [wf, wft, wu, wg, bfc, bfr,
                       sgu_g.astype(F32).reshape(1, D_HALF), sgu_b.astype(F32).reshape(1, D_HALF)]
    return pl.pallas_call(
        functools.partial(_proj_even_kernel, prompt=prompt),
        grid=(m // tm,),
        in_specs=[pl.BlockSpec((tm, d), lambda i: (i, 0)), _resident((1, d))] + [_resident(w.shape) for w in weights],
        out_specs=[s for _, s in outs],
        out_shape=[s for s, _ in outs],
        compiler_params=_params("parallel"),
        name="proj_even",
    )(x, g.astype(F32).reshape(1, d), *weights)


def _proj_odd_kernel(*refs, prompt):
    nw = 4 if prompt else 3
    x_ref, g_ref = refs[:2]
    w_refs = refs[2:2 + nw]
    wgb_ref, wgc_ref, wh_ref = refs[2 + nw:5 + nw]
    o_refs = refs[5 + nw:10 + nw]
    gb_ref, u_ref = refs[10 + nw:]
    hb = _rms(x_ref[...], g_ref[...]).astype(BF16)
    _emit_qkv(hb, w_refs, o_refs, prompt)
    gb_ref[...] = _dot(hb, wgb_ref[...])
    u_ref[...] = _dot(hb, wgc_ref[...]) * _dot(hb, wh_ref[...])


def _proj_odd(x, g, w_in, b, t, prompt):
    m, d = x.shape
    tm = _row_tile(t if prompt else m, 512)
    wb = w_in.astype(BF16)
    ws = [wb[:, i * D_HALF:(i + 1) * D_HALF] for i in range(6)]
    weights = _qkv_weights(ws[0], ws[1], ws[2], prompt) + ws[3:]
    outs = _qkv_outs(b, t, tm, prompt) + [_tok_out(m, tm, D_HALF, F32), _tok_out(m, tm, D_HALF, F32)]
    return pl.pallas_call(
        functools.partial(_proj_odd_kernel, prompt=prompt),
        grid=(m // tm,),
        in_specs=[pl.BlockSpec((tm, d), lambda i: (i, 0)), _resident((1, d))] + [_resident(w.shape) for w in weights],
        out_specs=[s for _, s in outs],
        out_shape=[s for s, _ in outs],
        compiler_params=_params("parallel"),
        name="proj_odd",
    )(x, g.astype(F32).reshape(1, d), *weights)


def _mix_mlp_kernel(x_ref, a_ref, b_ref, woa_ref, wob_ref, g_ref, wup_ref, wdn_ref, gf_ref, o_ref,
                    *, final_norm, ff_chunk):
    x1 = x_ref[...] + _dot(a_ref[...], woa_ref[...]) + _dot(b_ref[...], wob_ref[...])
    hb = _rms(x1, g_ref[...]).astype(BF16)
    y = None
    for c in range(wup_ref.shape[1] // ff_chunk):
        sl = slice(c * ff_chunk, (c + 1) * ff_chunk)
        up = jnp.maximum(_dot(hb, wup_ref[:, sl]), 0.0)
        down = _dot((up * up).astype(BF16), wdn_ref[sl, :])
        y = down if y is None else y + down
    out = x1 + y
    o_ref[...] = _rms(out, gf_ref[...]) if final_norm else out


def _mix_mlp(x, a, b, w_out, g_mlp, w_up, w_down, g_final, final_norm):
    m, d = x.shape
    tm = _row_tile(m, 512)
    wo = w_out.astype(BF16)
    weights = [wo[:D_HALF], wo[D_HALF:], g_mlp.astype(F32).reshape(1, d), w_up.astype(BF16),
               w_down.astype(BF16), g_final.astype(F32).reshape(1, d)]
    row = lambda i: (i, 0)
    return pl.pallas_call(
        functools.partial(_mix_mlp_kernel, final_norm=final_norm, ff_chunk=1024),
        grid=(m // tm,),
        in_specs=[pl.BlockSpec((tm, d), row), pl.BlockSpec((tm, D_HALF), row), pl.BlockSpec((tm, D_HALF), row)]
        + [_resident(w.shape) for w in weights],
        out_specs=pl.BlockSpec((tm, d), row),
        out_shape=jax.ShapeDtypeStruct((m, d), F32),
        compiler_params=_params("parallel"),
        name="mix_mlp",
    )(x, a, b, *weights)


def _sgu_kernel(vn_ref, u_ref, w_ref, mask_ref, bias_ref, o_ref):
    lo = lax.broadcasted_iota(jnp.int32, (1, PAIR), 1) < HEAD_DIM
    keep = mask_ref[...] > 0.0
    for gp in range(N_PAIRS):
        we = jnp.where(keep, w_ref[2 * gp], 0.0).astype(BF16)
        wo = jnp.where(keep, w_ref[2 * gp + 1], 0.0).astype(BF16)
        cols = slice(gp * PAIR, (gp + 1) * PAIR)
        for c in range(vn_ref.shape[0] // CHUNK):
            rows = slice(c * CHUNK, (c + 1) * CHUNK)
            vn2 = vn_ref[rows, cols]
            mixed = jnp.where(lo, _dot(we, vn2), _dot(wo, vn2)) + bias_ref[:, cols]
            o_ref[rows, cols] = (u_ref[rows, cols] * mixed).astype(BF16)


def _sgu(vnb, u, w, mask, bias):
    m = vnb.shape[0]
    assert m % CHUNK == 0
    tm = _row_tile(m, 512)
    row = lambda i: (i, 0)
    return pl.pallas_call(
        _sgu_kernel,
        grid=(m // tm,),
        in_specs=[pl.BlockSpec((tm, D_HALF), row), pl.BlockSpec((tm, D_HALF), row),
                  _resident(w.shape), _resident(mask.shape), _resident(bias.shape)],
        out_specs=pl.BlockSpec((tm, D_HALF), row),
        out_shape=jax.ShapeDtypeStruct((m, D_HALF), BF16),
        compiler_params=_params("parallel"),
        name="sgu",
    )(vnb, u, w, mask, bias)


def _conv_kernel(u_ref, gb_ref, prev_ref, w_ref, o_ref, new_ref, carry_ref):
    i = pl.program_id(1)

    @pl.when(i == 0)
    def _():
        carry_ref[...] = jnp.zeros_like(carry_ref)
        carry_ref[6:8, :] = prev_ref[0]

    u = u_ref[0]
    tt = u.shape[0]
    prev = carry_ref[...]
    r = lax.broadcasted_iota(jnp.int32, (tt, 1), 0)
    u1 = pltpu.roll(u, 1, 0)
    u2 = pltpu.roll(u, 2, 0)
    p1 = jnp.broadcast_to(prev[7:8], u.shape)
    p2 = jnp.where(r == 0, jnp.broadcast_to(prev[6:7], u.shape), p1)
    u1 = jnp.where(r == 0, p1, u1)
    u2 = jnp.where(r <= 1, p2, u2)
    conv = w_ref[0:1] * u2 + w_ref[1:2] * u1 + w_ref[2:3] * u
    o_ref[0] = (gb_ref[0] * conv).astype(BF16)
    if tt >= 8:
        tail = u[tt - 8:tt]
    else:
        tail = jnp.concatenate([prev[tt:8], u], axis=0)
    carry_ref[...] = tail
    new_ref[0] = tail[6:8]


def _conv(u, gb, prev, conv_w):
    b, t, d = u.shape
    tt = _row_tile(t, 1024)
    blk = lambda bi, i: (bi, i, 0)
    return pl.pallas_call(
        _conv_kernel,
        grid=(b, t // tt),
        in_specs=[pl.BlockSpec((1, tt, d), blk), pl.BlockSpec((1, tt, d), blk),
                  pl.BlockSpec((1, CONV_W - 1, d), lambda bi, i: (bi, 0, 0)), _resident((CONV_W, d))],
        out_specs=[pl.BlockSpec((1, tt, d), blk), pl.BlockSpec((1, CONV_W - 1, d), lambda bi, i: (bi, 0, 0))],
        out_shape=[jax.ShapeDtypeStruct((b, t, d), BF16), jax.ShapeDtypeStruct((b, CONV_W - 1, d), F32)],
        scratch_shapes=[pltpu.VMEM((8, d), F32)],
        compiler_params=_params("parallel", "arbitrary"),
        name="short_conv",
    )(u, gb, prev.astype(F32), conv_w.astype(F32))


AUG_ONE0 = 6


def _split3(x):
    hi = x.astype(BF16).astype(F32)
    r = x - hi
    mid = r.astype(BF16).astype(F32)
    lo = (r - mid).astype(BF16).astype(F32)
    return hi, mid, lo


def _cumf_kernel(lfc_ref, lfr_ref, k_ref, fr_ref, kcat_ref):
    t = lfc_ref.shape[0]
    lower = _tri(CHUNK, "row_ge_col")
    upper = _tri(CHUNK, "row_le_col")
    src = lax.broadcasted_iota(jnp.int32, (LANES, LANES), 0)
    dst = lax.broadcasted_iota(jnp.int32, (LANES, LANES), 1)
    lane = lax.broadcasted_iota(jnp.int32, (1, LANES), 1)
    ones = jnp.where((lane >= AUG_ONE0) & (lane < AUG_ONE0 + 3), 1.0, 0.0)
    sel = [[jnp.where(((src == 2 * hp) & (dst == c)) | ((src == 2 * hp + 1) & (dst == 3 + c)), 1.0, 0.0).astype(BF16)
            for c in range(3)] for hp in range(N_PAIRS)]

    def body(c, carry):
        cc, cr = carry
        o = pl.multiple_of(c * CHUNK, CHUNK)
        fc = _dot_f32(lower, lfc_ref[pl.ds(o, CHUNK), :]) + cc
        fr = _dot_f32(lfr_ref[0, :, pl.ds(o, CHUNK)], upper) + cr
        fr_ref[0, :, pl.ds(o, CHUNK)] = fr
        parts = [p.astype(BF16) for p in _split3(fc)]
        for hp in range(N_PAIRS):
            aug = _dot(parts[0], sel[hp][0]) + _dot(parts[1], sel[hp][1]) + _dot(parts[2], sel[hp][2]) + ones
            kcat_ref[pl.ds(o, CHUNK), 2 * hp * LANES:(2 * hp + 1) * LANES] = k_ref[pl.ds(o, CHUNK), hp * LANES:(hp + 1) * LANES]
            kcat_ref[pl.ds(o, CHUNK), (2 * hp + 1) * LANES:(2 * hp + 2) * LANES] = aug.astype(BF16)
        return fc[CHUNK - 1:CHUNK, :], fr[:, CHUNK - 1:CHUNK]

    lax.fori_loop(0, t // CHUNK, body, (jnp.zeros((1, LANES), F32), jnp.zeros((N_HEADS, 1), F32)))


def _cumf(lfc, lfr, kb, b, t):
    return pl.pallas_call(
        _cumf_kernel,
        grid=(b,),
        in_specs=[pl.BlockSpec((t, LANES), lambda i: (i, 0)), pl.BlockSpec((1, N_HEADS, t), lambda i: (i, 0, 0)),
                  pl.BlockSpec((t, D_HALF), lambda i: (i, 0))],
        out_specs=[pl.BlockSpec((1, N_HEADS, t), lambda i: (i, 0, 0)),
                   pl.BlockSpec((t, 2 * D_HALF), lambda i: (i, 0))],
        out_shape=[jax.ShapeDtypeStruct((b, N_HEADS, t), F32),
                   jax.ShapeDtypeStruct((b * t, 2 * D_HALF), BF16)],
        compiler_params=_params("parallel"),
        name="cum_logf",
    )(lfc, lfr, kb)


def _stack_pair_t(qt):
    lo = lax.broadcasted_iota(jnp.int32, (PAIR, 1), 0) < HEAD_DIM
    zero = jnp.zeros_like(qt)
    return jnp.concatenate([jnp.where(lo, qt, zero), jnp.where(lo, zero, qt)], axis=1)


def _unstack_pair_t(acc, tq):
    lo = lax.broadcasted_iota(jnp.int32, (PAIR, 1), 0) < HEAD_DIM
    return jnp.where(lo, acc[:, :tq], acc[:, tq:]).T


def _causal_mask_t(tk, tq, key0, q0, strict):
    kpos = key0 + lax.broadcasted_iota(jnp.int32, (tk, tq), 0)
    qpos = q0 + lax.broadcasted_iota(jnp.int32, (tk, tq), 1)
    ok = (kpos < qpos) if strict else (kpos <= qpos)
    return jnp.concatenate([ok, ok], axis=1)


def _fox_kernel(qt_ref, kcat_ref, vt_ref, fq_ref, o_ref, *, tq, tk):
    i = pl.program_id(2)
    qst = _stack_pair_t(qt_ref[0])
    fq = fq_ref[0, 0]
    r = lax.broadcasted_iota(jnp.int32, (LANES, tq), 0)

    def qaug_half(parity):
        parts = _split3(fq[parity:parity + 1])
        x = jnp.where((r >= 3 * parity) & (r < 3 * parity + 3), -1.0, 0.0)
        for c in range(3):
            x = jnp.where(r == AUG_ONE0 + c, jnp.broadcast_to(parts[c], (LANES, tq)), x)
        return x

    qaug = jnp.concatenate([qaug_half(0), qaug_half(1)], axis=1).astype(BF16)
    qcat = jnp.concatenate([qst, qaug], axis=0)

    def block(j, carry, masked):
        m, l, acc = carry
        o = pl.multiple_of(j * tk, tk)
        s = _dot(kcat_ref[pl.ds(o, tk), :], qcat)
        if masked:
            s = jnp.where(_causal_mask_t(tk, tq, o, i * tq, False), s, NEG_BIG)
        m_new = jnp.maximum(m, jnp.max(s, axis=0, keepdims=True))
        alpha = jnp.exp(m - m_new)
        p = jnp.exp(s - m_new)
        l = alpha * l + jnp.sum(p, axis=0, keepdims=True)
        acc = alpha * acc + _dot(vt_ref[0, :, pl.ds(o, tk)], p.astype(BF16))
        return m_new, l, acc

    init = (jnp.full((1, 2 * tq), NEG_BIG, F32), jnp.zeros((1, 2 * tq), F32), jnp.zeros((PAIR, 2 * tq), F32))
    per = tq // tk
    carry = lax.fori_loop(0, i * per, functools.partial(block, masked=False), init)
    for d in range(per):
        carry = block(i * per + d, carry, True)
    _, l, acc = carry
    o_ref[...] = _unstack_pair_t(acc / l, tq).astype(BF16)


def _attn_specs(t, tq, nq):
    qt_spec = pl.BlockSpec((1, PAIR, tq), lambda bi, hp, i: (bi, hp, i))
    k_spec = pl.BlockSpec((t, PAIR), lambda bi, hp, i: (bi, hp))
    vt_spec = pl.BlockSpec((1, PAIR, t), lambda bi, hp, i: (bi, hp, 0))
    o_spec = pl.BlockSpec((tq, PAIR), lambda bi, hp, i: (bi * nq + i, hp))
    return qt_spec, k_spec, vt_spec, o_spec


def _fox_attn(qtb, kcat, vtb, frow, b, t):
    tq = tk = min(t, FOX_BLOCK)
    nq = t // tq
    qt_spec, k_spec, vt_spec, o_spec = _attn_specs(t, tq, nq)
    return pl.pallas_call(
        functools.partial(_fox_kernel, tq=tq, tk=tk),
        grid=(b, N_PAIRS, nq),
        in_specs=[qt_spec, pl.BlockSpec((t, 2 * PAIR), lambda bi, hp, i: (bi, hp)), vt_spec,
                  pl.BlockSpec((1, 1, 2, tq), lambda bi, hp, i: (bi, hp, 0, i))],
        out_specs=o_spec,
        out_shape=jax.ShapeDtypeStruct((b * t, D_HALF), BF16),
        compiler_params=_params("parallel", "parallel", "arbitrary"),
        name="fox_attn",
    )(qtb, kcat, vtb, frow.reshape(b, N_PAIRS, 2, t))


def _sb_block_t(k, vt, qst, run, acc, tri, mask):
    z = _dot(k, qst)
    lk = -(jnp.maximum(z, 0.0) + jnp.log(1.0 + jnp.exp(-jnp.abs(z))))
    if mask is not None:
        lk = jnp.where(mask, lk, 0.0)
    hi = lk.astype(BF16)
    lo = (lk - hi.astype(F32)).astype(BF16)
    later = _dot(tri, hi) + _dot(tri, lo) + run
    a = jnp.exp(z + lk + later)
    if mask is not None:
        a = jnp.where(mask, a, 0.0)
    acc = acc + _dot(vt, a.astype(BF16))
    run = run + jnp.sum(lk, axis=0, keepdims=True)
    return run, acc


def _sb_kernel(qt_ref, k_ref, vt_ref, o_ref, *, tq, tk):
    i = pl.program_id(2)
    qst = _stack_pair_t(qt_ref[0])
    tri = _tri(tk, "row_lt_col").astype(BF16)
    per = tq // tk
    carry = (jnp.zeros((1, 2 * tq), F32), jnp.zeros((PAIR, 2 * tq), F32))
    for d in reversed(range(per)):
        o = pl.multiple_of((i * per + d) * tk, tk)
        carry = _sb_block_t(k_ref[pl.ds(o, tk), :], vt_ref[0, :, pl.ds(o, tk)], qst, *carry, tri,
                            _causal_mask_t(tk, tq, o, i * tq, True))

    def cond(c):
        n, run, _ = c
        return (n < i * per) & (jnp.max(run) > SB_LOG_FLOOR)

    def body(c):
        n, run, acc = c
        o = pl.multiple_of((i * per - 1 - n) * tk, tk)
        run, acc = _sb_block_t(k_ref[pl.ds(o, tk), :], vt_ref[0, :, pl.ds(o, tk)], qst, run, acc, tri, None)
        return n + 1, run, acc

    _, _, acc = lax.while_loop(cond, body, (jnp.int32(0),) + carry)
    o_ref[...] = _unstack_pair_t(acc, tq).astype(BF16)


def _sb_attn(qtb, kb, vtb, b, t):
    tq = tk = min(t, SB_BLOCK)
    nq = t // tq
    qt_spec, k_spec, vt_spec, o_spec = _attn_specs(t, tq, nq)
    return pl.pallas_call(
        functools.partial(_sb_kernel, tq=tq, tk=tk),
        grid=(b, N_PAIRS, nq),
        in_specs=[qt_spec, k_spec, vt_spec],
        out_specs=o_spec,
        out_shape=jax.ShapeDtypeStruct((b * t, D_HALF), BF16),
        compiler_params=_params("parallel", "parallel", "arbitrary"),
        name="sb_attn",
    )(qtb, kb, vtb)


def _page_specs(block, n):
    nd = len(block) - 1
    return [pl.BlockSpec(block, functools.partial(lambda b, g, pt, p: (pt[b, g * n + p],) + (0,) * nd, p=p))
            for p in range(n)]


def _page_specs_rev(block, n, groups):
    nd = len(block) - 1

    def imap(b, g, pt, p):
        grp = groups - jnp.maximum(g, 1)
        return (pt[b, grp * n + p],) + (0,) * nd

    return [pl.BlockSpec(block, functools.partial(imap, p=p)) for p in range(n)]


def _cumf_dec_kernel(pt_ref, *refs, n):
    pages, lfn_ref, fp_ref, fn_ref, carry_ref = refs[:n], refs[n], refs[n + 1], refs[n + 2], refs[n + 3]
    g = pl.program_id(1)
    upper = _tri(PAGE, "row_le_col")

    @pl.when(g == 0)
    def _():
        carry_ref[...] = jnp.zeros_like(carry_ref)

    carry = carry_ref[:, 0:1]
    for p in range(n):
        f = _dot_f32(pages[p][0], upper) + carry
        fp_ref[0, :, p * PAGE:(p + 1) * PAGE] = f
        carry = f[:, PAGE - 1:PAGE]
    carry_ref[...] = jnp.broadcast_to(carry, carry_ref.shape)
    fn_ref[0] = _dot_f32(lfn_ref[0], upper) + carry


def _cumf_dec(page_table, lf_pool_t, lf_new):
    nb, n_pages = page_table.shape
    n = _row_tile(n_pages, LOGF_PAGES_PER_STEP)
    groups = n_pages // n
    return pl.pallas_call(
        functools.partial(_cumf_dec_kernel, n=n),
        grid_spec=pltpu.PrefetchScalarGridSpec(
            num_scalar_prefetch=1,
            grid=(nb, groups),
            in_specs=_page_specs((1, N_HEADS, PAGE), n) + [pl.BlockSpec((1, N_HEADS, PAGE), lambda b, g, pt: (b, 0, 0))],
            out_specs=[pl.BlockSpec((1, N_HEADS, n * PAGE), lambda b, g, pt: (b, 0, g)),
                       pl.BlockSpec((1, N_HEADS, PAGE), lambda b, g, pt: (b, 0, 0))],
            scratch_shapes=[pltpu.VMEM((N_HEADS, LANES), F32)]),
        out_shape=[jax.ShapeDtypeStruct((nb, N_HEADS, n_pages * PAGE), F32),
                   jax.ShapeDtypeStruct((nb, N_HEADS, PAGE), F32)],
        compiler_params=_params("parallel", "arbitrary"),
        name="cum_logf_paged",
    )(page_table, *([lf_pool_t] * n), lf_new)


def _block_diag_q(q):
    t = q.shape[0]
    rep = jnp.concatenate([q.astype(F32)] * N_HEADS, axis=0)
    rh = lax.broadcasted_iota(jnp.int32, (N_HEADS * t, D_HALF), 0) // t
    lh = lax.broadcasted_iota(jnp.int32, (N_HEADS * t, D_HALF), 1) // HEAD_DIM
    return jnp.where(rh == lh, rep, 0.0).astype(BF16)


def _collapse_heads(acc, t):
    rh = lax.broadcasted_iota(jnp.int32, (N_HEADS * t, D_HALF), 0) // t
    lh = lax.broadcasted_iota(jnp.int32, (N_HEADS * t, D_HALF), 1) // HEAD_DIM
    masked = jnp.where(rh == lh, acc, 0.0)
    out = masked[0:t]
    for h in range(1, N_HEADS):
        out = out + masked[h * t:(h + 1) * t]
    return out


def _rep_heads(x, t):
    return jnp.concatenate([jnp.broadcast_to(x[h:h + 1], (t, x.shape[1])) for h in range(N_HEADS)], axis=0)


def _fox_dec_kernel(pt_ref, *refs, nt):
    n = PAGES_PER_STEP
    kp, vp = refs[:n], refs[n:2 * n]
    q_ref, kn_ref, vn_ref, fp_ref, fn_ref, o_ref, ks_ref, vs_ref, m_ref, l_ref, acc_ref = refs[2 * n:]
    g = pl.program_id(1)
    rows = N_HEADS * nt

    @pl.when(g == 0)
    def _():
        m_ref[...] = jnp.full_like(m_ref, NEG_BIG)
        l_ref[...] = jnp.zeros_like(l_ref)
        acc_ref[...] = jnp.zeros_like(acc_ref)

    for p in range(n):
        ks_ref[:, p * PAGE:(p + 1) * PAGE] = kp[p][0].astype(BF16)
        vs_ref[:, p * PAGE:(p + 1) * PAGE] = vp[p][0].astype(BF16)

    qbd = _block_diag_q(q_ref[0])
    fnew = _rep_heads(fn_ref[0], nt)
    tpos = lax.broadcasted_iota(jnp.int32, (rows, LANES), 0) % nt
    lane = lax.broadcasted_iota(jnp.int32, (rows, LANES), 1)
    fq = jnp.sum(jnp.where(lane == tpos, fnew, 0.0), axis=1, keepdims=True)

    def update(s, vt):
        m_old = m_ref[:, 0:1]
        m_new = jnp.maximum(m_old, jnp.max(s, axis=1, keepdims=True))
        alpha = jnp.exp(m_old - m_new)
        p_ = jnp.exp(s - m_new)
        l_ref[...] = jnp.broadcast_to(alpha * l_ref[:, 0:1] + jnp.sum(p_, axis=1, keepdims=True), l_ref.shape)
        acc_ref[...] = alpha * acc_ref[...] + _dot_nt(p_.astype(BF16), vt)
        m_ref[...] = jnp.broadcast_to(m_new, m_ref.shape)

    s = _dot(qbd, ks_ref[...]) + (fq - _rep_heads(fp_ref[0], nt))
    update(s, vs_ref[...])

    @pl.when(g == pl.num_programs(1) - 1)
    def _():
        sn = _dot(qbd, kn_ref[0]) + (fq - fnew)
        sn = jnp.where(lane <= tpos, sn, NEG_BIG)
        update(sn, vn_ref[0])
        o_ref[0] = _collapse_heads(acc_ref[...] / l_ref[:, 0:1], nt).astype(BF16)


def _fox_dec(page_table, k_pool, v_pool, qb, knt, vnt, f_past, f_new):
    nb, n_pages = page_table.shape
    nt = qb.shape[1]
    n = PAGES_PER_STEP
    groups = n_pages // n
    rows = N_HEADS * nt
    seq = lambda b, g, pt: (b, 0, 0)
    return pl.pallas_call(
        functools.partial(_fox_dec_kernel, nt=nt),
        grid_spec=pltpu.PrefetchScalarGridSpec(
            num_scalar_prefetch=1,
            grid=(nb, groups),
            in_specs=_page_specs((1, D_HALF, PAGE), n) + _page_specs((1, D_HALF, PAGE), n)
            + [pl.BlockSpec((1, nt, D_HALF), seq), pl.BlockSpec((1, D_HALF, PAGE), seq),
               pl.BlockSpec((1, D_HALF, PAGE), seq),
               pl.BlockSpec((1, N_HEADS, n * PAGE), lambda b, g, pt: (b, 0, g)),
               pl.BlockSpec((1, N_HEADS, PAGE), seq)],
            out_specs=pl.BlockSpec((1, nt, D_HALF), seq),
            scratch_shapes=[pltpu.VMEM((D_HALF, n * PAGE), BF16), pltpu.VMEM((D_HALF, n * PAGE), BF16),
                            pltpu.VMEM((rows, LANES), F32), pltpu.VMEM((rows, LANES), F32),
                            pltpu.VMEM((rows, D_HALF), F32)]),
        out_shape=jax.ShapeDtypeStruct((nb, nt, D_HALF), BF16),
        compiler_params=_params("parallel", "arbitrary"),
        name="fox_decode",
    )(page_table, *([k_pool] * n), *([v_pool] * n), qb, knt, vnt, f_past, f_new)


def _sb_block(qs, kt, vt, run, acc, strict_tri, mask):
    z = _dot(qs, kt)
    lk = -(jnp.maximum(z, 0.0) + jnp.log(1.0 + jnp.exp(-jnp.abs(z))))
    if mask is not None:
        lk = jnp.where(mask, lk, 0.0)
    hi = lk.astype(BF16)
    lo = (lk - hi.astype(F32)).astype(BF16)
    later = _dot(hi, strict_tri) + _dot(lo, strict_tri) + run
    a = jnp.exp(z + lk + later)
    if mask is not None:
        a = jnp.where(mask, a, 0.0)
    acc = acc + _dot_nt(a.astype(BF16), vt)
    run = run + jnp.sum(lk, axis=1, keepdims=True)
    return run, acc


def _sb_dec_kernel(pt_ref, *refs, nt):
    n = PAGES_PER_STEP
    kp, vp = refs[:n], refs[n:2 * n]
    q_ref, kn_ref, vn_ref, o_ref, run_ref, acc_ref = refs[2 * n:]
    g = pl.program_id(1)
    rows = N_HEADS * nt
    tri = _tri(PAGE, "row_gt_col").astype(BF16)
    qbd = _block_diag_q(q_ref[0])

    @pl.when(g == 0)
    def _():
        tpos = lax.broadcasted_iota(jnp.int32, (rows, LANES), 0) % nt
        lane = lax.broadcasted_iota(jnp.int32, (rows, LANES), 1)
        run, acc = _sb_block(qbd, kn_ref[0], vn_ref[0], jnp.zeros((rows, 1), F32),
                             jnp.zeros((rows, D_HALF), F32), tri, lane < tpos)
        run_ref[...] = jnp.broadcast_to(run, run_ref.shape)
        acc_ref[...] = acc

    @pl.when(g > 0)
    def _():
        run = run_ref[:, 0:1]
        acc = acc_ref[...]
        for p in reversed(range(n)):
            run, acc = _sb_block(qbd, kp[p][0].astype(BF16), vp[p][0].astype(BF16), run, acc, tri, None)
        run_ref[...] = jnp.broadcast_to(run, run_ref.shape)
        acc_ref[...] = acc

    @pl.when(g == pl.num_programs(1) - 1)
    def _():
        o_ref[0] = _collapse_heads(acc_ref[...], nt).astype(BF16)


def _sb_dec(page_table, k_pool, v_pool, qb, knt, vnt):
    nb, n_pages = page_table.shape
    nt = qb.shape[1]
    n = PAGES_PER_STEP
    groups = n_pages // n
    rows = N_HEADS * nt
    seq = lambda b, g, pt: (b, 0, 0)
    return pl.pallas_call(
        functools.partial(_sb_dec_kernel, nt=nt),
        grid_spec=pltpu.PrefetchScalarGridSpec(
            num_scalar_prefetch=1,
            grid=(nb, groups + 1),
            in_specs=_page_specs_rev((1, D_HALF, PAGE), n, groups) + _page_specs_rev((1, D_HALF, PAGE), n, groups)
            + [pl.BlockSpec((1, nt, D_HALF), seq), pl.BlockSpec((1, D_HALF, PAGE), seq),
               pl.BlockSpec((1, D_HALF, PAGE), seq)],
            out_specs=pl.BlockSpec((1, nt, D_HALF), seq),
            scratch_shapes=[pltpu.VMEM((rows, LANES), F32), pltpu.VMEM((rows, D_HALF), F32)]),
        out_shape=jax.ShapeDtypeStruct((nb, nt, D_HALF), BF16),
        compiler_params=_params("parallel", "arbitrary"),
        name="sb_decode",
    )(page_table, *([k_pool] * n), *([v_pool] * n), qb, knt, vnt)


def _new_feat_major(kb, b, t):
    return jnp.pad(kb.reshape(b, t, D_HALF).transpose(0, 2, 1), ((0, 0), (0, 0), (0, PAGE - t)))


def _heads_last(kt, b, t):
    return kt.reshape(b, N_HEADS, HEAD_DIM, t).transpose(0, 3, 1, 2)[None]


def _trunk(x, p, cache):
    b, t, d = x.shape
    m = b * t
    x2 = x.reshape(m, d)
    prompt = cache is None

    proj = _proj_even(x2, p["g_mix"][0], p["w_in_even"][0], p["b_forget"][0], p["sgu_g"][0], p["sgu_b"][0],
                      b, t, prompt)
    lfc, lfr, u, vn, vnb = proj[5:]
    w_s, b_s = p["w_spatial"][0].astype(F32), p["b_spatial"][0].astype(F32)
    if prompt:
        qt, kb, kt, vt, vtb = proj[:5]
        frow, kcat = _cumf(lfc, lfr, kb, b, t)
        a = _fox_attn(qt, kcat, vtb, frow, b, t)
        mask = np.tril(np.ones((CHUNK, CHUNK), np.float32))
        bias = jnp.repeat(b_s.T, HEAD_DIM, axis=1)
        w_mix = w_s
        even_rows = (_heads_last(kt, b, t), _heads_last(vt, b, t), lfr.transpose(0, 2, 1)[None], None)
    else:
        qb, k, kb, v, vb = proj[:5]
        pt = cache["page_table"]
        lf_new = jnp.pad(lfr.reshape(N_HEADS, b, t).transpose(1, 0, 2), ((0, 0), (0, 0), (0, PAGE - t)))
        f_past, f_new = _cumf_dec(pt, cache["fox_logf_t"], lf_new)
        a = _fox_dec(pt, cache["fox_k"], cache["fox_v"], qb.reshape(b, t, D_HALF),
                     _new_feat_major(kb, b, t), _new_feat_major(vb, b, t), f_past, f_new).reshape(m, D_HALF)
        reps = CHUNK // t
        idx = np.arange(CHUNK)
        mask = ((idx[:, None] // t == idx[None, :] // t) & (idx[None, :] % t <= idx[:, None] % t)).astype(np.float32)
        w_mix = jnp.tile(w_s[:, :t, :t], (1, reps, reps))
        bias = jnp.tile(jnp.repeat(b_s.T[:t], HEAD_DIM, axis=1), (reps, 1))
        hd = lambda z: z.reshape(1, b, t, N_HEADS, HEAD_DIM)
        even_rows = (hd(k), hd(v), lfc[:, :N_HEADS].reshape(1, b, t, N_HEADS), vn.reshape(1, b, t, D_HALF))
    ob = _sgu(vnb, u, w_mix, jnp.asarray(mask), bias)
    x2 = _mix_mlp(x2, a, ob, p["w_out_even"][0], p["g_mlp"][0], p["w_up"][0], p["w_down"][0], p["g_final"], False)

    proj = _proj_odd(x2, p["g_mix"][1], p["w_in_odd"][0], b, t, prompt)
    gb, uc = proj[5:]
    if prompt:
        qt, kb, kt, vt, vtb = proj[:5]
        a = _sb_attn(qt, kb, vtb, b, t)
        prev = jnp.zeros((b, CONV_W - 1, D_HALF), F32)
        kv_rows = (_heads_last(kt, b, t), _heads_last(vt, b, t))
    else:
        qb, k, kb, v, vb = proj[:5]
        a = _sb_dec(cache["page_table"], cache["sb_k"], cache["sb_v"], qb.reshape(b, t, D_HALF),
                    _new_feat_major(kb, b, t), _new_feat_major(vb, b, t)).reshape(m, D_HALF)
        prev = cache["conv"]
        kv_rows = (k.reshape(1, b, t, N_HEADS, HEAD_DIM), v.reshape(1, b, t, N_HEADS, HEAD_DIM))
    od, new_conv = _conv(uc.reshape(b, t, D_HALF), gb.reshape(b, t, D_HALF), prev, p["conv_w"][0])
    y = _mix_mlp(x2, a, od.reshape(m, D_HALF), p["w_out_odd"][0], p["g_mlp"][1], p["w_up"][1], p["w_down"][1],
                 p["g_final"], True)
    return y.reshape(b, t, d), even_rows, kv_rows + (new_conv[None],)


def _pool_feat_major(cache):
    n_pool = cache.shape[1]
    return cache[0].transpose(0, 2, 3, 1).reshape(n_pool, D_HALF, PAGE)


def kernel(x_prompt, x_sample, cache_fox_k, cache_fox_v, cache_fox_logf, cache_sb_k, cache_sb_v, state_conv,
           page_table, g_mix, g_mlp, g_final, w_up, w_down, w_in_even, b_forget, sgu_g, sgu_b, w_spatial,
           b_spatial, w_out_even, w_in_odd, conv_w, w_out_odd):
    p = dict(g_mix=g_mix, g_mlp=g_mlp, g_final=g_final, w_up=w_up, w_down=w_down, w_in_even=w_in_even,
             b_forget=b_forget, sgu_g=sgu_g, sgu_b=sgu_b, w_spatial=w_spatial, b_spatial=b_spatial,
             w_out_even=w_out_even, w_in_odd=w_in_odd, conv_w=conv_w, w_out_odd=w_out_odd)
    y_p, (pk, pv, plf, _), (psk, psv, pconv) = _trunk(x_prompt, p, None)
    cache = dict(page_table=page_table,
                 fox_k=_pool_feat_major(cache_fox_k), fox_v=_pool_feat_major(cache_fox_v),
                 fox_logf_t=cache_fox_logf[0].transpose(0, 2, 1),
                 sb_k=_pool_feat_major(cache_sb_k), sb_v=_pool_feat_major(cache_sb_v),
                 conv=state_conv[0])
    y_s, (sk, sv, slf, svn), (ssk, ssv, sconv) = _trunk(x_sample, p, cache)
    return (y_p, y_s, pk, pv, plf, psk, psv, pconv, sk, sv, slf, ssk, ssv, sconv, svn)
```

```python
import functools

import numpy as np
import jax
import jax.numpy as jnp
from jax import lax
from jax.experimental import pallas as pl
from jax.experimental.pallas import tpu as pltpu

F32 = jnp.float32
BF16 = jnp.bfloat16

HEAD_DIM = 64
N_HEADS = 8
PAIR = 2 * HEAD_DIM
N_PAIRS = N_HEADS // 2
D_HALF = N_HEADS * HEAD_DIM
CHUNK = 128
PAGE = 128
CONV_W = 3
EPS = 1e-6
QK_SCALE = HEAD_DIM ** -0.5
NEG_BIG = -1e30
LANES = 128
VMEM_LIMIT = 56 * 1024 * 1024
PAGES_PER_STEP = 16
LOGF_PAGES_PER_STEP = 64
SB_PAGES_PER_FETCH = 2
FOX_BLOCK = 512
FOX_CHAINS = 2
SB_BLOCK = 256
SB_LOG_FLOOR = -110.0

NT_DIMS = (((1,), (1,)), ((), ()))


def _dot(a, b):
    return jnp.dot(a, b, preferred_element_type=F32)


def _dot_nt(a, b):
    return lax.dot_general(a, b, NT_DIMS, preferred_element_type=F32)


def _dot_f32(a, b):
    return jnp.dot(a, b, preferred_element_type=F32, precision=lax.Precision.HIGHEST)


def _rms(x, g):
    return x * lax.rsqrt(jnp.mean(x * x, axis=-1, keepdims=True) + EPS) * g


def _log_sigmoid(x):
    return jnp.minimum(x, 0.0) - jnp.log1p(jnp.exp(-jnp.abs(x)))


def _params(*sem):
    return pltpu.CompilerParams(dimension_semantics=sem, vmem_limit_bytes=VMEM_LIMIT)


def _resident(shape):
    nd = len(shape)
    return pl.BlockSpec(shape, lambda *_: (0,) * nd, pipeline_mode=pl.Buffered(1))


def _row_tile(m, cap):
    t = min(m, cap)
    assert m % t == 0
    return t


def _tri(n, kind):
    r = lax.broadcasted_iota(jnp.int32, (n, n), 0)
    c = lax.broadcasted_iota(jnp.int32, (n, n), 1)
    keep = {"row_le_col": r <= c, "row_ge_col": r >= c, "row_gt_col": r > c, "row_lt_col": r < c}[kind]
    return jnp.where(keep, 1.0, 0.0).astype(F32)


def _emit_qkv(hb, w_refs, o_refs, prompt):
    if prompt:
        wqt, wk, wkt, wvt = w_refs
        qt_ref, kb_ref, kt_ref, vt_ref, vtb_ref = o_refs
        qt_ref[0] = (_dot_nt(wqt[...], hb) * QK_SCALE).astype(BF16)
        kb_ref[...] = _dot(hb, wk[...]).astype(BF16)
        kt_ref[0] = _dot_nt(wkt[...], hb)
        vt = _dot_nt(wvt[...], hb)
        vt_ref[0] = vt
        vtb_ref[0] = vt.astype(BF16)
    else:
        wq, wk, wv = w_refs
        q_ref, k_ref, kb_ref, v_ref, vb_ref = o_refs
        q_ref[...] = (_dot(hb, wq[...]) * QK_SCALE).astype(BF16)
        k = _dot(hb, wk[...])
        k_ref[...] = k
        kb_ref[...] = k.astype(BF16)
        v = _dot(hb, wv[...])
        v_ref[...] = v
        vb_ref[...] = v.astype(BF16)


def _qkv_weights(wq, wk, wv, prompt):
    return [wq.T, wk, wk.T, wv.T] if prompt else [wq, wk, wv]


def _tok_out(m, tm, w, dt):
    return jax.ShapeDtypeStruct((m, w), dt), pl.BlockSpec((tm, w), lambda i: (i, 0))


def _feat_out(b, t, tm, w, dt):
    nt = t // tm
    return jax.ShapeDtypeStruct((b, w, t), dt), pl.BlockSpec((1, w, tm), lambda i: (i // nt, 0, i % nt))


def _qkv_outs(b, t, tm, prompt):
    m = b * t
    if prompt:
        return [_feat_out(b, t, tm, D_HALF, BF16), _tok_out(m, tm, D_HALF, BF16), _feat_out(b, t, tm, D_HALF, F32),
                _feat_out(b, t, tm, D_HALF, F32), _feat_out(b, t, tm, D_HALF, BF16)]
    return [_tok_out(m, tm, D_HALF, BF16), _tok_out(m, tm, D_HALF, F32), _tok_out(m, tm, D_HALF, BF16),
            _tok_out(m, tm, D_HALF, F32), _tok_out(m, tm, D_HALF, BF16)]


def _proj_even_kernel(*refs, prompt):
    nw = 4 if prompt else 3
    x_ref, g_ref = refs[:2]
    w_refs = refs[2:2 + nw]
    wf_ref, wft_ref, wu_ref, wg_ref, bfc_ref, bfr_ref, sg_ref, sb_ref = refs[2 + nw:10 + nw]
    o_refs = refs[10 + nw:15 + nw]
    lfc_ref, lfr_ref, u_ref, vn_ref, vnb_ref = refs[15 + nw:]
    hb = _rms(x_ref[...], g_ref[...]).astype(BF16)
    _emit_qkv(hb, w_refs, o_refs, prompt)
    lfc_ref[...] = _log_sigmoid(_dot(hb, wf_ref[...]) + bfc_ref[...])
    lfr = _log_sigmoid(_dot_nt(wft_ref[...], hb)[:N_HEADS] + bfr_ref[...])
    if prompt:
        lfr_ref[0] = lfr
    else:
        lfr_ref[...] = lfr
    u_ref[...] = _dot(hb, wu_ref[...])
    vg = _dot(hb, wg_ref[...])
    mu = jnp.mean(vg, axis=-1, keepdims=True)
    vc = vg - mu
    var = jnp.mean(vc * vc, axis=-1, keepdims=True)
    vn = vc * lax.rsqrt(var + EPS) * sg_ref[...] + sb_ref[...]
    vn_ref[...] = vn
    vnb_ref[...] = vn.astype(BF16)


def _proj_even(x, g, w_in, b_forget, sgu_g, sgu_b, b, t, prompt):
    m, d = x.shape
    tm = _row_tile(t if prompt else m, 512)
    wb = w_in.astype(BF16)
    o = 3 * D_HALF
    qkv_w = _qkv_weights(wb[:, :D_HALF], wb[:, D_HALF:2 * D_HALF], wb[:, 2 * D_HALF:o], prompt)
    wf = jnp.pad(wb[:, o:o + N_HEADS], ((0, 0), (0, LANES - N_HEADS)))
    wft = jnp.pad(wb[:, o:o + N_HEADS].T, ((0, 16 - N_HEADS), (0, 0)))
    wu, wg = wb[:, o + N_HEADS:o + N_HEADS + D_HALF], wb[:, o + N_HEADS + D_HALF:]
    bfc = jnp.pad(b_forget.astype(F32), (0, LANES - N_HEADS)).reshape(1, LANES)
    bfr = b_forget.astype(F32).reshape(N_HEADS, 1)
    lfr_out = (_feat_out(b, t, tm, N_HEADS, F32) if prompt else
               (jax.ShapeDtypeStruct((N_HEADS, m), F32), pl.BlockSpec((N_HEADS, tm), lambda i: (0, i))))
    outs = (_qkv_outs(b, t, tm, prompt)
            + [_tok_out(m, tm, LANES, F32), lfr_out,
               _tok_out(m, tm, D_HALF, F32), _tok_out(m, tm, D_HALF, F32), _tok_out(m, tm, D_HALF, BF16)])
    weights = qkv_w + [wf, wft, wu, wg, bfc, bfr,
                       sgu_g.astype(F32).reshape(1, D_HALF), sgu_b.astype(F32).reshape(1, D_HALF)]
    return pl.pallas_call(
        functools.partial(_proj_even_kernel, prompt=prompt),
        grid=(m // tm,),
        in_specs=[pl.BlockSpec((tm, d), lambda i: (i, 0)), _resident((1, d))] + [_resident(w.shape) for w in weights],
        out_specs=[s for _, s in outs],
        out_shape=[s for s, _ in outs],
        compiler_params=_params("parallel"),
        name="proj_even",
    )(x, g.astype(F32).reshape(1, d), *weights)


def _proj_odd_kernel(*refs, prompt):
    nw = 4 if prompt else 3
    x_ref, g_ref = refs[:2]
    w_refs = refs[2:2 + nw]
    wgb_ref, wgc_ref, wh_ref = refs[2 + nw:5 + nw]
    o_refs = refs[5 + nw:10 + nw]
    gb_ref, u_ref = refs[10 + nw:]
    hb = _rms(x_ref[...], g_ref[...]).astype(BF16)
    _emit_qkv(hb, w_refs, o_refs, prompt)
    gb_ref[...] = _dot(hb, wgb_ref[...])
    u_ref[...] = _dot(hb, wgc_ref[...]) * _dot(hb, wh_ref[...])


def _proj_odd(x, g, w_in, b, t, prompt):
    m, d = x.shape
    tm = _row_tile(t if prompt else m, 512)
    wb = w_in.astype(BF16)
    ws = [wb[:, i * D_HALF:(i + 1) * D_HALF] for i in range(6)]
    weights = _qkv_weights(ws[0], ws[1], ws[2], prompt) + ws[3:]
    outs = _qkv_outs(b, t, tm, prompt) + [_tok_out(m, tm, D_HALF, F32), _tok_out(m, tm, D_HALF, F32)]
    return pl.pallas_call(
        functools.partial(_proj_odd_kernel, prompt=prompt),
        grid=(m // tm,),
        in_specs=[pl.BlockSpec((tm, d), lambda i: (i, 0)), _resident((1, d))] + [_resident(w.shape) for w in weights],
        out_specs=[s for _, s in outs],
        out_shape=[s for s, _ in outs],
        compiler_params=_params("parallel"),
        name="proj_odd",
    )(x, g.astype(F32).reshape(1, d), *weights)


def _mix_mlp_kernel(x_ref, a_ref, b_ref, woa_ref, wob_ref, g_ref, wup_ref, wdn_ref, gf_ref, o_ref,
                    *, final_norm, ff_chunk):
    x1 = x_ref[...] + _dot(a_ref[...], woa_ref[...]) + _dot(b_ref[...], wob_ref[...])
    hb = _rms(x1, g_ref[...]).astype(BF16)
    y = None
    for c in range(wup_ref.shape[1] // ff_chunk):
        sl = slice(c * ff_chunk, (c + 1) * ff_chunk)
        up = jnp.maximum(_dot(hb, wup_ref[:, sl]), 0.0)
        down = _dot((up * up).astype(BF16), wdn_ref[sl, :])
        y = down if y is None else y + down
    out = x1 + y
    o_ref[...] = _rms(out, gf_ref[...]) if final_norm else out


def _mix_mlp(x, a, b, w_out, g_mlp, w_up, w_down, g_final, final_norm):
    m, d = x.shape
    tm = _row_tile(m, 512)
    wo = w_out.astype(BF16)
    weights = [wo[:D_HALF], wo[D_HALF:], g_mlp.astype(F32).reshape(1, d), w_up.astype(BF16),
               w_down.astype(BF16), g_final.astype(F32).reshape(1, d)]
    row = lambda i: (i, 0)
    return pl.pallas_call(
        functools.partial(_mix_mlp_kernel, final_norm=final_norm, ff_chunk=1024),
        grid=(m // tm,),
        in_specs=[pl.BlockSpec((tm, d), row), pl.BlockSpec((tm, D_HALF), row), pl.BlockSpec((tm, D_HALF), row)]
        + [_resident(w.shape) for w in weights],
        out_specs=pl.BlockSpec((tm, d), row),
        out_shape=jax.ShapeDtypeStruct((m, d), F32),
        compiler_params=_params("parallel"),
        name="mix_mlp",
    )(x, a, b, *weights)


def _sgu_kernel(vn_ref, u_ref, w_ref, mask_ref, bias_ref, o_ref):
    lo = lax.broadcasted_iota(jnp.int32, (1, PAIR), 1) < HEAD_DIM
    keep = mask_ref[...] > 0.0
    for gp in range(N_PAIRS):
        we = jnp.where(keep, w_ref[2 * gp], 0.0).astype(BF16)
        wo = jnp.where(keep, w_ref[2 * gp + 1], 0.0).astype(BF16)
        cols = slice(gp * PAIR, (gp + 1) * PAIR)
        for c in range(vn_ref.shape[0] // CHUNK):
            rows = slice(c * CHUNK, (c + 1) * CHUNK)
            vn2 = vn_ref[rows, cols]
            mixed = jnp.where(lo, _dot(we, vn2), _dot(wo, vn2)) + bias_ref[:, cols]
            o_ref[rows, cols] = (u_ref[rows, cols] * mixed).astype(BF16)


def _sgu(vnb, u, w, mask, bias):
    m = vnb.shape[0]
    assert m % CHUNK == 0
    tm = _row_tile(m, 512)
    row = lambda i: (i, 0)
    return pl.pallas_call(
        _sgu_kernel,
        grid=(m // tm,),
        in_specs=[pl.BlockSpec((tm, D_HALF), row), pl.BlockSpec((tm, D_HALF), row),
                  _resident(w.shape), _resident(mask.shape), _resident(bias.shape)],
        out_specs=pl.BlockSpec((tm, D_HALF), row),
        out_shape=jax.ShapeDtypeStruct((m, D_HALF), BF16),
        compiler_params=_params("parallel"),
        name="sgu",
    )(vnb, u, w, mask, bias)


def _conv_kernel(u_ref, gb_ref, prev_ref, w_ref, o_ref, new_ref, carry_ref):
    i = pl.program_id(1)

    @pl.when(i == 0)
    def _():
        carry_ref[...] = jnp.zeros_like(carry_ref)
        carry_ref[6:8, :] = prev_ref[0]

    u = u_ref[0]
    tt = u.shape[0]
    prev = carry_ref[...]
    r = lax.broadcasted_iota(jnp.int32, (tt, 1), 0)
    u1 = pltpu.roll(u, 1, 0)
    u2 = pltpu.roll(u, 2, 0)
    p1 = jnp.broadcast_to(prev[7:8], u.shape)
    p2 = jnp.where(r == 0, jnp.broadcast_to(prev[6:7], u.shape), p1)
    u1 = jnp.where(r == 0, p1, u1)
    u2 = jnp.where(r <= 1, p2, u2)
    conv = w_ref[0:1] * u2 + w_ref[1:2] * u1 + w_ref[2:3] * u
    o_ref[0] = (gb_ref[0] * conv).astype(BF16)
    if tt >= 8:
        tail = u[tt - 8:tt]
    else:
        tail = jnp.concatenate([prev[tt:8], u], axis=0)
    carry_ref[...] = tail
    new_ref[0] = tail[6:8]


def _conv(u, gb, prev, conv_w):
    b, t, d = u.shape
    tt = _row_tile(t, 1024)
    blk = lambda bi, i: (bi, i, 0)
    return pl.pallas_call(
        _conv_kernel,
        grid=(b, t // tt),
        in_specs=[pl.BlockSpec((1, tt, d), blk), pl.BlockSpec((1, tt, d), blk),
                  pl.BlockSpec((1, CONV_W - 1, d), lambda bi, i: (bi, 0, 0)), _resident((CONV_W, d))],
        out_specs=[pl.BlockSpec((1, tt, d), blk), pl.BlockSpec((1, CONV_W - 1, d), lambda bi, i: (bi, 0, 0))],
        out_shape=[jax.ShapeDtypeStruct((b, t, d), BF16), jax.ShapeDtypeStruct((b, CONV_W - 1, d), F32)],
        scratch_shapes=[pltpu.VMEM((8, d), F32)],
        compiler_params=_params("parallel", "arbitrary"),
        name="short_conv",
    )(u, gb, prev.astype(F32), conv_w.astype(F32))


AUG_ONE0 = 6


def _split3(x):
    hi = x.astype(BF16).astype(F32)
    r = x - hi
    mid = r.astype(BF16).astype(F32)
    lo = (r - mid).astype(BF16).astype(F32)
    return hi, mid, lo


def _cumf_kernel(lfc_ref, lfr_ref, k_ref, fr_ref, kcat_ref):
    t = lfc_ref.shape[0]
    lower = _tri(CHUNK, "row_ge_col")
    upper = _tri(CHUNK, "row_le_col")
    src = lax.broadcasted_iota(jnp.int32, (LANES, LANES), 0)
    dst = lax.broadcasted_iota(jnp.int32, (LANES, LANES), 1)
    lane = lax.broadcasted_iota(jnp.int32, (1, LANES), 1)
    ones = jnp.where((lane >= AUG_ONE0) & (lane < AUG_ONE0 + 3), 1.0, 0.0)
    sel = [[jnp.where(((src == 2 * hp) & (dst == c)) | ((src == 2 * hp + 1) & (dst == 3 + c)), 1.0, 0.0).astype(BF16)
            for c in range(3)] for hp in range(N_PAIRS)]

    def body(c, carry):
        cc, cr = carry
        o = pl.multiple_of(c * CHUNK, CHUNK)
        fc = _dot_f32(lower, lfc_ref[pl.ds(o, CHUNK), :]) + cc
        fr = _dot_f32(lfr_ref[0, :, pl.ds(o, CHUNK)], upper) + cr
        fr_ref[0, :, pl.ds(o, CHUNK)] = fr
        parts = [p.astype(BF16) for p in _split3(fc)]
        for hp in range(N_PAIRS):
            aug = _dot(parts[0], sel[hp][0]) + _dot(parts[1], sel[hp][1]) + _dot(parts[2], sel[hp][2]) + ones
            kcat_ref[pl.ds(o, CHUNK), 2 * hp * LANES:(2 * hp + 1) * LANES] = k_ref[pl.ds(o, CHUNK), hp * LANES:(hp + 1) * LANES]
            kcat_ref[pl.ds(o, CHUNK), (2 * hp + 1) * LANES:(2 * hp + 2) * LANES] = aug.astype(BF16)
        return fc[CHUNK - 1:CHUNK, :], fr[:, CHUNK - 1:CHUNK]

    lax.fori_loop(0, t // CHUNK, body, (jnp.zeros((1, LANES), F32), jnp.zeros((N_HEADS, 1), F32)))


def _cumf(lfc, lfr, kb, b, t):
    return pl.pallas_call(
        _cumf_kernel,
        grid=(b,),
        in_specs=[pl.BlockSpec((t, LANES), lambda i: (i, 0)), pl.BlockSpec((1, N_HEADS, t), lambda i: (i, 0, 0)),
                  pl.BlockSpec((t, D_HALF), lambda i: (i, 0))],
        out_specs=[pl.BlockSpec((1, N_HEADS, t), lambda i: (i, 0, 0)),
                   pl.BlockSpec((t, 2 * D_HALF), lambda i: (i, 0))],
        out_shape=[jax.ShapeDtypeStruct((b, N_HEADS, t), F32),
                   jax.ShapeDtypeStruct((b * t, 2 * D_HALF), BF16)],
        compiler_params=_params("parallel"),
        name="cum_logf",
    )(lfc, lfr, kb)


def _stack_pair_t(qt):
    lo = lax.broadcasted_iota(jnp.int32, (PAIR, 1), 0) < HEAD_DIM
    zero = jnp.zeros_like(qt)
    return jnp.concatenate([jnp.where(lo, qt, zero), jnp.where(lo, zero, qt)], axis=1)


def _unstack_pair_t(acc, tq):
    lo = lax.broadcasted_iota(jnp.int32, (PAIR, 1), 0) < HEAD_DIM
    return jnp.where(lo, acc[:, :tq], acc[:, tq:]).T


def _causal_mask_t(tk, tq, key0, q0, strict):
    kpos = key0 + lax.broadcasted_iota(jnp.int32, (tk, tq), 0)
    qpos = q0 + lax.broadcasted_iota(jnp.int32, (tk, tq), 1)
    ok = (kpos < qpos) if strict else (kpos <= qpos)
    return jnp.concatenate([ok, ok], axis=1)


def _fox_kernel(qt_ref, kcat_ref, vt_ref, fq_ref, o_ref, *, tq, tk, nc):
    i = pl.program_id(2)
    r = lax.broadcasted_iota(jnp.int32, (LANES, tq), 0)

    def qcat_of(c):
        qst = _stack_pair_t(qt_ref[0, c * PAIR:(c + 1) * PAIR, :])
        fq = fq_ref[0, 0, 2 * c:2 * c + 2, :]

        def qaug_half(parity):
            parts = _split3(fq[parity:parity + 1])
            x = jnp.where((r >= 3 * parity) & (r < 3 * parity + 3), -1.0, 0.0)
            for k in range(3):
                x = jnp.where(r == AUG_ONE0 + k, jnp.broadcast_to(parts[k], (LANES, tq)), x)
            return x

        qaug = jnp.concatenate([qaug_half(0), qaug_half(1)], axis=1).astype(BF16)
        return jnp.concatenate([qst, qaug], axis=0)

    qcats = [qcat_of(c) for c in range(nc)]

    def block_one(c, o, carry, masked):
        m, l, acc = carry
        s = _dot(kcat_ref[pl.ds(o, tk), c * 2 * PAIR:(c + 1) * 2 * PAIR], qcats[c])
        if masked:
            s = jnp.where(_causal_mask_t(tk, tq, o, i * tq, False), s, NEG_BIG)
        m_new = jnp.maximum(m, jnp.max(s, axis=0, keepdims=True))
        alpha = jnp.exp(m - m_new)
        p = jnp.exp(s - m_new)
        l = alpha * l + jnp.sum(p, axis=0, keepdims=True)
        acc = alpha * acc + _dot(vt_ref[0, c * PAIR:(c + 1) * PAIR, pl.ds(o, tk)], p.astype(BF16))
        return m_new, l, acc

    def block(j, carries, masked):
        o = pl.multiple_of(j * tk, tk)
        return tuple(block_one(c, o, carries[c], masked) for c in range(nc))

    init = (jnp.full((1, 2 * tq), NEG_BIG, F32), jnp.zeros((1, 2 * tq), F32), jnp.zeros((PAIR, 2 * tq), F32))
    per = tq // tk
    carries = lax.fori_loop(0, i * per, functools.partial(block, masked=False), (init,) * nc)
    for d in range(per):
        carries = block(i * per + d, carries, True)
    for c in range(nc):
        _, l, acc = carries[c]
        o_ref[:, c * PAIR:(c + 1) * PAIR] = _unstack_pair_t(acc / l, tq).astype(BF16)


def _attn_specs(t, tq, nq):
    qt_spec = pl.BlockSpec((1, PAIR, tq), lambda bi, hp, i: (bi, hp, i))
    k_spec = pl.BlockSpec((t, PAIR), lambda bi, hp, i: (bi, hp))
    vt_spec = pl.BlockSpec((1, PAIR, t), lambda bi, hp, i: (bi, hp, 0))
    o_spec = pl.BlockSpec((tq, PAIR), lambda bi, hp, i: (bi * nq + i, hp))
    return qt_spec, k_spec, vt_spec, o_spec


def _fox_attn(qtb, kcat, vtb, frow, b, t):
    tq = tk = min(t, FOX_BLOCK)
    nq = t // tq
    nc = FOX_CHAINS
    return pl.pallas_call(
        functools.partial(_fox_kernel, tq=tq, tk=tk, nc=nc),
        grid=(b, N_PAIRS // nc, nq),
        in_specs=[pl.BlockSpec((1, nc * PAIR, tq), lambda bi, g, i: (bi, g, i)),
                  pl.BlockSpec((t, nc * 2 * PAIR), lambda bi, g, i: (bi, g)),
                  pl.BlockSpec((1, nc * PAIR, t), lambda bi, g, i: (bi, g, 0)),
                  pl.BlockSpec((1, 1, 2 * nc, tq), lambda bi, g, i: (bi, g, 0, i))],
        out_specs=pl.BlockSpec((tq, nc * PAIR), lambda bi, g, i: (bi * nq + i, g)),
        out_shape=jax.ShapeDtypeStruct((b * t, D_HALF), BF16),
        compiler_params=_params("parallel", "parallel", "arbitrary"),
        name="fox_attn",
    )(qtb, kcat, vtb, frow.reshape(b, N_PAIRS // nc, 2 * nc, t))


def _sb_block_t(k, vt, qst, run, acc, tri, mask):
    z = _dot(k, qst)
    lk = -(jnp.maximum(z, 0.0) + jnp.log(1.0 + jnp.exp(-jnp.abs(z))))
    if mask is not None:
        lk = jnp.where(mask, lk, 0.0)
    hi = lk.astype(BF16)
    lo = (lk - hi.astype(F32)).astype(BF16)
    later = _dot(tri, hi) + _dot(tri, lo) + run
    a = jnp.exp(z + lk + later)
    if mask is not None:
        a = jnp.where(mask, a, 0.0)
    acc = acc + _dot(vt, a.astype(BF16))
    run = run + jnp.sum(lk, axis=0, keepdims=True)
    return run, acc


def _sb_kernel(qt_ref, k_ref, vt_ref, o_ref, *, tq, tk):
    i = pl.program_id(2)
    qst = _stack_pair_t(qt_ref[0])
    tri = _tri(tk, "row_lt_col").astype(BF16)
    per = tq // tk
    carry = (jnp.zeros((1, 2 * tq), F32), jnp.zeros((PAIR, 2 * tq), F32))
    for d in reversed(range(per)):
        o = pl.multiple_of((i * per + d) * tk, tk)
        carry = _sb_block_t(k_ref[pl.ds(o, tk), :], vt_ref[0, :, pl.ds(o, tk)], qst, *carry, tri,
                            _causal_mask_t(tk, tq, o, i * tq, True))

    def cond(c):
        n, run, _ = c
        return (n < i * per) & (jnp.max(run) > SB_LOG_FLOOR)

    def body(c):
        n, run, acc = c
        o = pl.multiple_of((i * per - 1 - n) * tk, tk)
        run, acc = _sb_block_t(k_ref[pl.ds(o, tk), :], vt_ref[0, :, pl.ds(o, tk)], qst, run, acc, tri, None)
        return n + 1, run, acc

    _, _, acc = lax.while_loop(cond, body, (jnp.int32(0),) + carry)
    o_ref[...] = _unstack_pair_t(acc, tq).astype(BF16)


def _sb_attn(qtb, kb, vtb, b, t):
    tq = tk = min(t, SB_BLOCK)
    nq = t // tq
    qt_spec, k_spec, vt_spec, o_spec = _attn_specs(t, tq, nq)
    return pl.pallas_call(
        functools.partial(_sb_kernel, tq=tq, tk=tk),
        grid=(b, N_PAIRS, nq),
        in_specs=[qt_spec, k_spec, vt_spec],
        out_specs=o_spec,
        out_shape=jax.ShapeDtypeStruct((b * t, D_HALF), BF16),
        compiler_params=_params("parallel", "parallel", "arbitrary"),
        name="sb_attn",
    )(qtb, kb, vtb)


def _page_specs(block, n):
    nd = len(block) - 1
    return [pl.BlockSpec(block, functools.partial(lambda b, g, pt, p: (pt[b, g * n + p],) + (0,) * nd, p=p))
            for p in range(n)]


def _cumf_dec_kernel(pt_ref, *refs, n):
    pages, lfn_ref, fp_ref, fn_ref, carry_ref = refs[:n], refs[n], refs[n + 1], refs[n + 2], refs[n + 3]
    g = pl.program_id(1)
    upper = _tri(PAGE, "row_le_col")

    @pl.when(g == 0)
    def _():
        carry_ref[...] = jnp.zeros_like(carry_ref)

    stacked = jnp.concatenate([pages[p][0] for p in range(n)] + [lfn_ref[0]], axis=0)
    within = _dot_f32(stacked, upper)
    carry = carry_ref[:, 0:1]
    for p in range(n):
        f = within[p * N_HEADS:(p + 1) * N_HEADS] + carry
        fp_ref[0, :, p * PAGE:(p + 1) * PAGE] = f
        carry = f[:, PAGE - 1:PAGE]
    carry_ref[...] = jnp.broadcast_to(carry, carry_ref.shape)
    fn_ref[0] = within[n * N_HEADS:] + carry


def _cumf_dec(page_table, lf_pool_t, lf_new):
    nb, n_pages = page_table.shape
    n = _row_tile(n_pages, LOGF_PAGES_PER_STEP)
    groups = n_pages // n
    return pl.pallas_call(
        functools.partial(_cumf_dec_kernel, n=n),
        grid_spec=pltpu.PrefetchScalarGridSpec(
            num_scalar_prefetch=1,
            grid=(nb, groups),
            in_specs=_page_specs((1, N_HEADS, PAGE), n) + [pl.BlockSpec((1, N_HEADS, PAGE), lambda b, g, pt: (b, 0, 0))],
            out_specs=[pl.BlockSpec((1, N_HEADS, n * PAGE), lambda b, g, pt: (b, 0, g)),
                       pl.BlockSpec((1, N_HEADS, PAGE), lambda b, g, pt: (b, 0, 0))],
            scratch_shapes=[pltpu.VMEM((N_HEADS, LANES), F32)]),
        out_shape=[jax.ShapeDtypeStruct((nb, N_HEADS, n_pages * PAGE), F32),
                   jax.ShapeDtypeStruct((nb, N_HEADS, PAGE), F32)],
        compiler_params=_params("parallel", "arbitrary"),
        name="cum_logf_paged",
    )(page_table, *([lf_pool_t] * n), lf_new)


def _block_diag_q(q):
    t = q.shape[0]
    rep = jnp.concatenate([q.astype(F32)] * N_HEADS, axis=0)
    rh = lax.broadcasted_iota(jnp.int32, (N_HEADS * t, D_HALF), 0) // t
    lh = lax.broadcasted_iota(jnp.int32, (N_HEADS * t, D_HALF), 1) // HEAD_DIM
    return jnp.where(rh == lh, rep, 0.0).astype(BF16)


def _collapse_heads(acc, t):
    rh = lax.broadcasted_iota(jnp.int32, (N_HEADS * t, D_HALF), 0) // t
    lh = lax.broadcasted_iota(jnp.int32, (N_HEADS * t, D_HALF), 1) // HEAD_DIM
    masked = jnp.where(rh == lh, acc, 0.0)
    out = masked[0:t]
    for h in range(1, N_HEADS):
        out = out + masked[h * t:(h + 1) * t]
    return out


def _rep_heads(x, t):
    return jnp.concatenate([jnp.broadcast_to(x[h:h + 1], (t, x.shape[1])) for h in range(N_HEADS)], axis=0)


def _fox_dec_kernel(pt_ref, *refs, nt):
    n = PAGES_PER_STEP
    kp, vp = refs[:n], refs[n:2 * n]
    q_ref, kn_ref, vn_ref, fp_ref, fn_ref, o_ref, ks_ref, vs_ref, m_ref, l_ref, acc_ref = refs[2 * n:]
    g = pl.program_id(1)
    rows = N_HEADS * nt

    @pl.when(g == 0)
    def _():
        m_ref[...] = jnp.full_like(m_ref, NEG_BIG)
        l_ref[...] = jnp.zeros_like(l_ref)
        acc_ref[...] = jnp.zeros_like(acc_ref)

    for p in range(n):
        ks_ref[:, p * PAGE:(p + 1) * PAGE] = kp[p][0].astype(BF16)
        vs_ref[:, p * PAGE:(p + 1) * PAGE] = vp[p][0].astype(BF16)

    qbd = _block_diag_q(q_ref[0])
    fnew = _rep_heads(fn_ref[0], nt)
    tpos = lax.broadcasted_iota(jnp.int32, (rows, LANES), 0) % nt
    lane = lax.broadcasted_iota(jnp.int32, (rows, LANES), 1)
    fq = jnp.sum(jnp.where(lane == tpos, fnew, 0.0), axis=1, keepdims=True)

    def update(s, vt):
        m_old = m_ref[:, 0:1]
        m_new = jnp.maximum(m_old, jnp.max(s, axis=1, keepdims=True))
        alpha = jnp.exp(m_old - m_new)
        p_ = jnp.exp(s - m_new)
        l_ref[...] = jnp.broadcast_to(alpha * l_ref[:, 0:1] + jnp.sum(p_, axis=1, keepdims=True), l_ref.shape)
        acc_ref[...] = alpha * acc_ref[...] + _dot_nt(p_.astype(BF16), vt)
        m_ref[...] = jnp.broadcast_to(m_new, m_ref.shape)

    s = _dot(qbd, ks_ref[...]) + (fq - _rep_heads(fp_ref[0], nt))
    update(s, vs_ref[...])

    @pl.when(g == pl.num_programs(1) - 1)
    def _():
        sn = _dot(qbd, kn_ref[0]) + (fq - fnew)
        sn = jnp.where(lane <= tpos, sn, NEG_BIG)
        update(sn, vn_ref[0])
        o_ref[0] = _collapse_heads(acc_ref[...] / l_ref[:, 0:1], nt).astype(BF16)


def _fox_dec(page_table, k_pool, v_pool, qb, knt, vnt, f_past, f_new):
    nb, n_pages = page_table.shape
    nt = qb.shape[1]
    n = PAGES_PER_STEP
    groups = n_pages // n
    rows = N_HEADS * nt
    seq = lambda b, g, pt: (b, 0, 0)
    return pl.pallas_call(
        functools.partial(_fox_dec_kernel, nt=nt),
        grid_spec=pltpu.PrefetchScalarGridSpec(
            num_scalar_prefetch=1,
            grid=(nb, groups),
            in_specs=_page_specs((1, D_HALF, PAGE), n) + _page_specs((1, D_HALF, PAGE), n)
            + [pl.BlockSpec((1, nt, D_HALF), seq), pl.BlockSpec((1, D_HALF, PAGE), seq),
               pl.BlockSpec((1, D_HALF, PAGE), seq),
               pl.BlockSpec((1, N_HEADS, n * PAGE), lambda b, g, pt: (b, 0, g)),
               pl.BlockSpec((1, N_HEADS, PAGE), seq)],
            out_specs=pl.BlockSpec((1, nt, D_HALF), seq),
            scratch_shapes=[pltpu.VMEM((D_HALF, n * PAGE), BF16), pltpu.VMEM((D_HALF, n * PAGE), BF16),
                            pltpu.VMEM((rows, LANES), F32), pltpu.VMEM((rows, LANES), F32),
                            pltpu.VMEM((rows, D_HALF), F32)]),
        out_shape=jax.ShapeDtypeStruct((nb, nt, D_HALF), BF16),
        compiler_params=_params("parallel", "arbitrary"),
        name="fox_decode",
    )(page_table, *([k_pool] * n), *([v_pool] * n), qb, knt, vnt, f_past, f_new)


def _sb_block(qs, kt, vt, run, acc, strict_tri, mask):
    z = _dot(qs, kt)
    lk = -(jnp.maximum(z, 0.0) + jnp.log(1.0 + jnp.exp(-jnp.abs(z))))
    if mask is not None:
        lk = jnp.where(mask, lk, 0.0)
    hi = lk.astype(BF16)
    lo = (lk - hi.astype(F32)).astype(BF16)
    later = _dot(hi, strict_tri) + _dot(lo, strict_tri) + run
    a = jnp.exp(z + lk + later)
    if mask is not None:
        a = jnp.where(mask, a, 0.0)
    acc = acc + _dot_nt(a.astype(BF16), vt)
    run = run + jnp.sum(lk, axis=1, keepdims=True)
    return run, acc


def _sb_dec_kernel(pt_ref, q_ref, kn_ref, vn_ref, kpool_ref, vpool_ref, o_ref, kbuf, vbuf, sem, *, nt, n_pages, gp):
    b = pl.program_id(0)
    slot = b % 2
    rows = N_HEADS * nt
    n_groups = n_pages // gp
    tri = _tri(PAGE, "row_gt_col").astype(BF16)
    qbd = _block_diag_q(q_ref[0])

    def group_copies(seq, grp, slot_):
        out = []
        for p in range(gp):
            page = pt_ref[seq, n_pages - (grp + 1) * gp + p]
            out.append(pltpu.make_async_copy(kpool_ref.at[page], kbuf.at[slot_, p], sem.at[slot_, 0]))
            out.append(pltpu.make_async_copy(vpool_ref.at[page], vbuf.at[slot_, p], sem.at[slot_, 1]))
        return out

    def visit_group(slot_, run, acc):
        for p in reversed(range(gp)):
            run, acc = _sb_block(qbd, kbuf[slot_, p].astype(BF16), vbuf[slot_, p].astype(BF16), run, acc, tri, None)
        return run, acc

    @pl.when(b == 0)
    def _():
        for c in group_copies(0, 0, 0):
            c.start()

    @pl.when(b + 1 < pl.num_programs(0))
    def _():
        for c in group_copies(b + 1, 0, 1 - slot):
            c.start()

    tpos = lax.broadcasted_iota(jnp.int32, (rows, LANES), 0) % nt
    lane = lax.broadcasted_iota(jnp.int32, (rows, LANES), 1)
    run, acc = _sb_block(qbd, kn_ref[0], vn_ref[0], jnp.zeros((rows, 1), F32), jnp.zeros((rows, D_HALF), F32),
                         tri, lane < tpos)

    for c in group_copies(b, 0, slot):
        c.wait()
    run, acc = visit_group(slot, run, acc)

    def cond(c):
        g, run, _ = c
        return (g < n_groups) & (jnp.max(run) > SB_LOG_FLOOR)

    def body(c):
        g, run, acc = c
        copies = group_copies(b, g, slot)
        for cp in copies:
            cp.start()
        for cp in copies:
            cp.wait()
        run, acc = visit_group(slot, run, acc)
        return g + 1, run, acc

    _, _, acc = lax.while_loop(cond, body, (jnp.int32(1), run, acc))
    o_ref[0] = _collapse_heads(acc, nt).astype(BF16)


def _sb_dec(page_table, k_pool, v_pool, qb, knt, vnt):
    nb, n_pages = page_table.shape
    nt = qb.shape[1]
    gp = _row_tile(n_pages, SB_PAGES_PER_FETCH)
    seq = lambda b, pt: (b, 0, 0)
    return pl.pallas_call(
        functools.partial(_sb_dec_kernel, nt=nt, n_pages=n_pages, gp=gp),
        grid_spec=pltpu.PrefetchScalarGridSpec(
            num_scalar_prefetch=1,
            grid=(nb,),
            in_specs=[pl.BlockSpec((1, nt, D_HALF), seq), pl.BlockSpec((1, D_HALF, PAGE), seq),
                      pl.BlockSpec((1, D_HALF, PAGE), seq),
                      pl.BlockSpec(memory_space=pl.ANY), pl.BlockSpec(memory_space=pl.ANY)],
            out_specs=pl.BlockSpec((1, nt, D_HALF), seq),
            scratch_shapes=[pltpu.VMEM((2, gp, D_HALF, PAGE), F32), pltpu.VMEM((2, gp, D_HALF, PAGE), F32),
                            pltpu.SemaphoreType.DMA((2, 2))]),
        out_shape=jax.ShapeDtypeStruct((nb, nt, D_HALF), BF16),
        compiler_params=_params("arbitrary"),
        name="sb_decode",
    )(page_table, qb, knt, vnt, k_pool, v_pool)


def _new_feat_major(kb, b, t):
    return jnp.pad(kb.reshape(b, t, D_HALF).transpose(0, 2, 1), ((0, 0), (0, 0), (0, PAGE - t)))


def _heads_last(kt, b, t):
    return kt.reshape(b, N_HEADS, HEAD_DIM, t).transpose(0, 3, 1, 2)[None]


def _trunk(x, p, cache):
    b, t, d = x.shape
    m = b * t
    x2 = x.reshape(m, d)
    prompt = cache is None

    proj = _proj_even(x2, p["g_mix"][0], p["w_in_even"][0], p["b_forget"][0], p["sgu_g"][0], p["sgu_b"][0],
                      b, t, prompt)
    lfc, lfr, u, vn, vnb = proj[5:]
    w_s, b_s = p["w_spatial"][0].astype(F32), p["b_spatial"][0].astype(F32)
    if prompt:
        qt, kb, kt, vt, vtb = proj[:5]
        frow, kcat = _cumf(lfc, lfr, kb, b, t)
        a = _fox_attn(qt, kcat, vtb, frow, b, t)
        mask = np.tril(np.ones((CHUNK, CHUNK), np.float32))
        bias = jnp.repeat(b_s.T, HEAD_DIM, axis=1)
        w_mix = w_s
        even_rows = (_heads_last(kt, b, t), _heads_last(vt, b, t), lfr.transpose(0, 2, 1)[None], None)
    else:
        qb, k, kb, v, vb = proj[:5]
        pt = cache["page_table"]
        lf_new = jnp.pad(lfr.reshape(N_HEADS, b, t).transpose(1, 0, 2), ((0, 0), (0, 0), (0, PAGE - t)))
        f_past, f_new = _cumf_dec(pt, cache["fox_logf_t"], lf_new)
        a = _fox_dec(pt, cache["fox_k"], cache["fox_v"], qb.reshape(b, t, D_HALF),
                     _new_feat_major(kb, b, t), _new_feat_major(vb, b, t), f_past, f_new).reshape(m, D_HALF)
        reps = CHUNK // t
        idx = np.arange(CHUNK)
        mask = ((idx[:, None] // t == idx[None, :] // t) & (idx[None, :] % t <= idx[:, None] % t)).astype(np.float32)
        w_mix = jnp.tile(w_s[:, :t, :t], (1, reps, reps))
        bias = jnp.tile(jnp.repeat(b_s.T[:t], HEAD_DIM, axis=1), (reps, 1))
        hd = lambda z: z.reshape(1, b, t, N_HEADS, HEAD_DIM)
        even_rows = (hd(k), hd(v), lfc[:, :N_HEADS].reshape(1, b, t, N_HEADS), vn.reshape(1, b, t, D_HALF))
    ob = _sgu(vnb, u, w_mix, jnp.asarray(mask), bias)
    x2 = _mix_mlp(x2, a, ob, p["w_out_even"][0], p["g_mlp"][0], p["w_up"][0], p["w_down"][0], p["g_final"], False)

    proj = _proj_odd(x2, p["g_mix"][1], p["w_in_odd"][0], b, t, prompt)
    gb, uc = proj[5:]
    if prompt:
        qt, kb, kt, vt, vtb = proj[:5]
        a = _sb_attn(qt, kb, vtb, b, t)
        prev = jnp.zeros((b, CONV_W - 1, D_HALF), F32)
        kv_rows = (_heads_last(kt, b, t), _heads_last(vt, b, t))
    else:
        qb, k, kb, v, vb = proj[:5]
        a = _sb_dec(cache["page_table"], cache["sb_k"], cache["sb_v"], qb.reshape(b, t, D_HALF),
                    _new_feat_major(kb, b, t), _new_feat_major(vb, b, t)).reshape(m, D_HALF)
        prev = cache["conv"]
        kv_rows = (k.reshape(1, b, t, N_HEADS, HEAD_DIM), v.reshape(1, b, t, N_HEADS, HEAD_DIM))
    od, new_conv = _conv(uc.reshape(b, t, D_HALF), gb.reshape(b, t, D_HALF), prev, p["conv_w"][0])
    y = _mix_mlp(x2, a, od.reshape(m, D_HALF), p["w_out_odd"][0], p["g_mlp"][1], p["w_up"][1], p["w_down"][1],
                 p["g_final"], True)
    return y.reshape(b, t, d), even_rows, kv_rows + (new_conv[None],)


def _pool_feat_major(cache):
    n_pool = cache.shape[1]
    return cache[0].transpose(0, 2, 3, 1).reshape(n_pool, D_HALF, PAGE)


def kernel(x_prompt, x_sample, cache_fox_k, cache_fox_v, cache_fox_logf, cache_sb_k, cache_sb_v, state_conv,
           page_table, g_mix, g_mlp, g_final, w_up, w_down, w_in_even, b_forget, sgu_g, sgu_b, w_spatial,
           b_spatial, w_out_even, w_in_odd, conv_w, w_out_odd):
    p = dict(g_mix=g_mix, g_mlp=g_mlp, g_final=g_final, w_up=w_up, w_down=w_down, w_in_even=w_in_even,
             b_forget=b_forget, sgu_g=sgu_g, sgu_b=sgu_b, w_spatial=w_spatial, b_spatial=b_spatial,
             w_out_even=w_out_even, w_in_odd=w_in_odd, conv_w=conv_w, w_out_odd=w_out_odd)
    y_p, (pk, pv, plf, _), (psk, psv, pconv) = _trunk(x_prompt, p, None)
    cache = dict(page_table=page_table,
                 fox_k=_pool_feat_major(cache_fox_k), fox_v=_pool_feat_major(cache_fox_v),
                 fox_logf_t=cache_fox_logf[0].transpose(0, 2, 1),
                 sb_k=_pool_feat_major(cache_sb_k), sb_v=_pool_feat_major(cache_sb_v),
                 conv=state_conv[0])
    y_s, (sk, sv, slf, svn), (ssk, ssv, sconv) = _trunk(x_sample, p, cache)
    return (y_p, y_s, pk, pv, plf, psk, psv, pconv, sk, sv, slf, ssk, ssv, sconv, svn)
```

```python
import functools

import numpy as np
import jax
import jax.numpy as jnp
from jax import lax
from jax.experimental import pallas as pl
from jax.experimental.pallas import tpu as pltpu

F32 = jnp.float32
BF16 = jnp.bfloat16

HEAD_DIM = 64
N_HEADS = 8
PAIR = 2 * HEAD_DIM
N_PAIRS = N_HEADS // 2
D_HALF = N_HEADS * HEAD_DIM
CHUNK = 128
PAGE = 128
CONV_W = 3
EPS = 1e-6
QK_SCALE = HEAD_DIM ** -0.5
NEG_BIG = -1e30
LANES = 128
VMEM_LIMIT = 56 * 1024 * 1024
PAGES_PER_STEP = 32
LOGF_PAGES_PER_STEP = 64
SB_PAGES_PER_FETCH = 2
FOX_BLOCK = 1024
FOX_CHAINS = 2
SB_BLOCK = 256
SB_LOG_FLOOR = -110.0

NT_DIMS = (((1,), (1,)), ((), ()))


def _dot(a, b):
    return jnp.dot(a, b, preferred_element_type=F32)


def _dot_nt(a, b):
    return lax.dot_general(a, b, NT_DIMS, preferred_element_type=F32)


def _dot_f32(a, b):
    return jnp.dot(a, b, preferred_element_type=F32, precision=lax.Precision.HIGHEST)


def _rms(x, g):
    return x * lax.rsqrt(jnp.mean(x * x, axis=-1, keepdims=True) + EPS) * g


def _log_sigmoid(x):
    return jnp.minimum(x, 0.0) - jnp.log1p(jnp.exp(-jnp.abs(x)))


def _params(*sem):
    return pltpu.CompilerParams(dimension_semantics=sem, vmem_limit_bytes=VMEM_LIMIT)


def _resident(shape):
    nd = len(shape)
    return pl.BlockSpec(shape, lambda *_: (0,) * nd, pipeline_mode=pl.Buffered(1))


def _row_tile(m, cap):
    t = min(m, cap)
    assert m % t == 0
    return t


def _tri(n, kind):
    r = lax.broadcasted_iota(jnp.int32, (n, n), 0)
    c = lax.broadcasted_iota(jnp.int32, (n, n), 1)
    keep = {"row_le_col": r <= c, "row_ge_col": r >= c, "row_gt_col": r > c, "row_lt_col": r < c}[kind]
    return jnp.where(keep, 1.0, 0.0).astype(F32)


def _emit_qkv(hb, w_refs, o_refs, prompt):
    if prompt:
        wqt, wk, wkt, wvt = w_refs
        qt_ref, kb_ref, kt_ref, vt_ref, vtb_ref = o_refs
        qt_ref[0] = (_dot_nt(wqt[...], hb) * QK_SCALE).astype(BF16)
        kb_ref[...] = _dot(hb, wk[...]).astype(BF16)
        kt_ref[0] = _dot_nt(wkt[...], hb)
        vt = _dot_nt(wvt[...], hb)
        vt_ref[0] = vt
        vtb_ref[0] = vt.astype(BF16)
    else:
        wq, wk, wv = w_refs
        q_ref, k_ref, kb_ref, v_ref, vb_ref = o_refs
        q_ref[...] = (_dot(hb, wq[...]) * QK_SCALE).astype(BF16)
        k = _dot(hb, wk[...])
        k_ref[...] = k
        kb_ref[...] = k.astype(BF16)
        v = _dot(hb, wv[...])
        v_ref[...] = v
        vb_ref[...] = v.astype(BF16)


def _qkv_weights(wq, wk, wv, prompt):
    return [wq.T, wk, wk.T, wv.T] if prompt else [wq, wk, wv]


def _tok_out(m, tm, w, dt):
    return jax.ShapeDtypeStruct((m, w), dt), pl.BlockSpec((tm, w), lambda i: (i, 0))


def _feat_out(b, t, tm, w, dt):
    nt = t // tm
    return jax.ShapeDtypeStruct((b, w, t), dt), pl.BlockSpec((1, w, tm), lambda i: (i // nt, 0, i % nt))


def _qkv_outs(b, t, tm, prompt):
    m = b * t
    if prompt:
        return [_feat_out(b, t, tm, D_HALF, BF16), _tok_out(m, tm, D_HALF, BF16), _feat_out(b, t, tm, D_HALF, F32),
                _feat_out(b, t, tm, D_HALF, F32), _feat_out(b, t, tm, D_HALF, BF16)]
    return [_tok_out(m, tm, D_HALF, BF16), _tok_out(m, tm, D_HALF, F32), _tok_out(m, tm, D_HALF, BF16),
            _tok_out(m, tm, D_HALF, F32), _tok_out(m, tm, D_HALF, BF16)]


def _proj_even_kernel(*refs, prompt):
    nw = 4 if prompt else 3
    x_ref, g_ref = refs[:2]
    w_refs = refs[2:2 + nw]
    wf_ref, wft_ref, wu_ref, wg_ref, bfc_ref, bfr_ref, sg_ref, sb_ref = refs[2 + nw:10 + nw]
    o_refs = refs[10 + nw:15 + nw]
    lfc_ref, lfr_ref, u_ref, vn_ref, vnb_ref = refs[15 + nw:]
    hb = _rms(x_ref[...], g_ref[...]).astype(BF16)
    _emit_qkv(hb, w_refs, o_refs, prompt)
    lfc_ref[...] = _log_sigmoid(_dot(hb, wf_ref[...]) + bfc_ref[...])
    lfr = _log_sigmoid(_dot_nt(wft_ref[...], hb)[:N_HEADS] + bfr_ref[...])
    if prompt:
        lfr_ref[0] = lfr
    else:
        lfr_ref[...] = lfr
    u_ref[...] = _dot(hb, wu_ref[...])
    vg = _dot(hb, wg_ref[...])
    mu = jnp.mean(vg, axis=-1, keepdims=True)
    vc = vg - mu
    var = jnp.mean(vc * vc, axis=-1, keepdims=True)
    vn = vc * lax.rsqrt(var + EPS) * sg_ref[...] + sb_ref[...]
    vn_ref[...] = vn
    vnb_ref[...] = vn.astype(BF16)


def _proj_even(x, g, w_in, b_forget, sgu_g, sgu_b, b, t, prompt):
    m, d = x.shape
    tm = _row_tile(t if prompt else m, 512)
    wb = w_in.astype(BF16)
    o = 3 * D_HALF
    qkv_w = _qkv_weights(wb[:, :D_HALF], wb[:, D_HALF:2 * D_HALF], wb[:, 2 * D_HALF:o], prompt)
    wf = jnp.pad(wb[:, o:o + N_HEADS], ((0, 0), (0, LANES - N_HEADS)))
    wft = jnp.pad(wb[:, o:o + N_HEADS].T, ((0, 16 - N_HEADS), (0, 0)))
    wu, wg = wb[:, o + N_HEADS:o + N_HEADS + D_HALF], wb[:, o + N_HEADS + D_HALF:]
    bfc = jnp.pad(b_forget.astype(F32), (0, LANES - N_HEADS)).reshape(1, LANES)
    bfr = b_forget.astype(F32).reshape(N_HEADS, 1)
    lfr_out = (_feat_out(b, t, tm, N_HEADS, F32) if prompt else
               (jax.ShapeDtypeStruct((N_HEADS, m), F32), pl.BlockSpec((N_HEADS, tm), lambda i: (0, i))))
    outs = (_qkv_outs(b, t, tm, prompt)
            + [_tok_out(m, tm, LANES, F32), lfr_out,
               _tok_out(m, tm, D_HALF, F32), _tok_out(m, tm, D_HALF, F32), _tok_out(m, tm, D_HALF, BF16)])
    weights = qkv_w + [wf, wft, wu, wg, bfc, bfr,
                       sgu_g.astype(F32).reshape(1, D_HALF), sgu_b.astype(F32).reshape(1, D_HALF)]
    return pl.pallas_call(
        functools.partial(_proj_even_kernel, prompt=prompt),
        grid=(m // tm,),
        in_specs=[pl.BlockSpec((tm, d), lambda i: (i, 0)), _resident((1, d))] + [_resident(w.shape) for w in weights],
        out_specs=[s for _, s in outs],
        out_shape=[s for s, _ in outs],
        compiler_params=_params("parallel"),
        name="proj_even",
    )(x, g.astype(F32).reshape(1, d), *weights)


def _proj_odd_kernel(*refs, prompt):
    nw = 4 if prompt else 3
    x_ref, g_ref = refs[:2]
    w_refs = refs[2:2 + nw]
    wgb_ref, wgc_ref, wh_ref = refs[2 + nw:5 + nw]
    o_refs = refs[5 + nw:10 + nw]
    gb_ref, u_ref = refs[10 + nw:]
    hb = _rms(x_ref[...], g_ref[...]).astype(BF16)
    _emit_qkv(hb, w_refs, o_refs, prompt)
    gb_ref[...] = _dot(hb, wgb_ref[...])
    u_ref[...] = _dot(hb, wgc_ref[...]) * _dot(hb, wh_ref[...])


def _proj_odd(x, g, w_in, b, t, prompt):
    m, d = x.shape
    tm = _row_tile(t if prompt else m, 512)
    wb = w_in.astype(BF16)
    ws = [wb[:, i * D_HALF:(i + 1) * D_HALF] for i in range(6)]
    weights = _qkv_weights(ws[0], ws[1], ws[2], prompt) + ws[3:]
    outs = _qkv_outs(b, t, tm, prompt) + [_tok_out(m, tm, D_HALF, F32), _tok_out(m, tm, D_HALF, F32)]
    return pl.pallas_call(
        functools.partial(_proj_odd_kernel, prompt=prompt),
        grid=(m // tm,),
        in_specs=[pl.BlockSpec((tm, d), lambda i: (i, 0)), _resident((1, d))] + [_resident(w.shape) for w in weights],
        out_specs=[s for _, s in outs],
        out_shape=[s for s, _ in outs],
        compiler_params=_params("parallel"),
        name="proj_odd",
    )(x, g.astype(F32).reshape(1, d), *weights)


def _mix_mlp_kernel(x_ref, a_ref, b_ref, woa_ref, wob_ref, g_ref, wup_ref, wdn_ref, gf_ref, o_ref,
                    *, final_norm, ff_chunk):
    x1 = x_ref[...] + _dot(a_ref[...], woa_ref[...]) + _dot(b_ref[...], wob_ref[...])
    hb = _rms(x1, g_ref[...]).astype(BF16)
    y = None
    for c in range(wup_ref.shape[1] // ff_chunk):
        sl = slice(c * ff_chunk, (c + 1) * ff_chunk)
        up = jnp.maximum(_dot(hb, wup_ref[:, sl]), 0.0)
        down = _dot((up * up).astype(BF16), wdn_ref[sl, :])
        y = down if y is None else y + down
    out = x1 + y
    o_ref[...] = _rms(out, gf_ref[...]) if final_norm else out


def _mix_mlp(x, a, b, w_out, g_mlp, w_up, w_down, g_final, final_norm):
    m, d = x.shape
    tm = _row_tile(m, 512)
    wo = w_out.astype(BF16)
    weights = [wo[:D_HALF], wo[D_HALF:], g_mlp.astype(F32).reshape(1, d), w_up.astype(BF16),
               w_down.astype(BF16), g_final.astype(F32).reshape(1, d)]
    row = lambda i: (i, 0)
    return pl.pallas_call(
        functools.partial(_mix_mlp_kernel, final_norm=final_norm, ff_chunk=1024),
        grid=(m // tm,),
        in_specs=[pl.BlockSpec((tm, d), row), pl.BlockSpec((tm, D_HALF), row), pl.BlockSpec((tm, D_HALF), row)]
        + [_resident(w.shape) for w in weights],
        out_specs=pl.BlockSpec((tm, d), row),
        out_shape=jax.ShapeDtypeStruct((m, d), F32),
        compiler_params=_params("parallel"),
        name="mix_mlp",
    )(x, a, b, *weights)


def _sgu_kernel(vn_ref, u_ref, w_ref, mask_ref, bias_ref, o_ref):
    lo = lax.broadcasted_iota(jnp.int32, (1, PAIR), 1) < HEAD_DIM
    keep = mask_ref[...] > 0.0
    for gp in range(N_PAIRS):
        we = jnp.where(keep, w_ref[2 * gp], 0.0).astype(BF16)
        wo = jnp.where(keep, w_ref[2 * gp + 1], 0.0).astype(BF16)
        cols = slice(gp * PAIR, (gp + 1) * PAIR)
        for c in range(vn_ref.shape[0] // CHUNK):
            rows = slice(c * CHUNK, (c + 1) * CHUNK)
            vn2 = vn_ref[rows, cols]
            mixed = jnp.where(lo, _dot(we, vn2), _dot(wo, vn2)) + bias_ref[:, cols]
            o_ref[rows, cols] = (u_ref[rows, cols] * mixed).astype(BF16)


def _sgu(vnb, u, w, mask, bias):
    m = vnb.shape[0]
    assert m % CHUNK == 0
    tm = _row_tile(m, 512)
    row = lambda i: (i, 0)
    return pl.pallas_call(
        _sgu_kernel,
        grid=(m // tm,),
        in_specs=[pl.BlockSpec((tm, D_HALF), row), pl.BlockSpec((tm, D_HALF), row),
                  _resident(w.shape), _resident(mask.shape), _resident(bias.shape)],
        out_specs=pl.BlockSpec((tm, D_HALF), row),
        out_shape=jax.ShapeDtypeStruct((m, D_HALF), BF16),
        compiler_params=_params("parallel"),
        name="sgu",
    )(vnb, u, w, mask, bias)


def _conv_kernel(u_ref, gb_ref, prev_ref, w_ref, o_ref, new_ref, carry_ref):
    i = pl.program_id(1)

    @pl.when(i == 0)
    def _():
        carry_ref[...] = jnp.zeros_like(carry_ref)
        carry_ref[6:8, :] = prev_ref[0]

    u = u_ref[0]
    tt = u.shape[0]
    prev = carry_ref[...]
    r = lax.broadcasted_iota(jnp.int32, (tt, 1), 0)
    u1 = pltpu.roll(u, 1, 0)
    u2 = pltpu.roll(u, 2, 0)
    p1 = jnp.broadcast_to(prev[7:8], u.shape)
    p2 = jnp.where(r == 0, jnp.broadcast_to(prev[6:7], u.shape), p1)
    u1 = jnp.where(r == 0, p1, u1)
    u2 = jnp.where(r <= 1, p2, u2)
    conv = w_ref[0:1] * u2 + w_ref[1:2] * u1 + w_ref[2:3] * u
    o_ref[0] = (gb_ref[0] * conv).astype(BF16)
    if tt >= 8:
        tail = u[tt - 8:tt]
    else:
        tail = jnp.concatenate([prev[tt:8], u], axis=0)
    carry_ref[...] = tail
    new_ref[0] = tail[6:8]


def _conv(u, gb, prev, conv_w):
    b, t, d = u.shape
    tt = _row_tile(t, 1024)
    blk = lambda bi, i: (bi, i, 0)
    return pl.pallas_call(
        _conv_kernel,
        grid=(b, t // tt),
        in_specs=[pl.BlockSpec((1, tt, d), blk), pl.BlockSpec((1, tt, d), blk),
                  pl.BlockSpec((1, CONV_W - 1, d), lambda bi, i: (bi, 0, 0)), _resident((CONV_W, d))],
        out_specs=[pl.BlockSpec((1, tt, d), blk), pl.BlockSpec((1, CONV_W - 1, d), lambda bi, i: (bi, 0, 0))],
        out_shape=[jax.ShapeDtypeStruct((b, t, d), BF16), jax.ShapeDtypeStruct((b, CONV_W - 1, d), F32)],
        scratch_shapes=[pltpu.VMEM((8, d), F32)],
        compiler_params=_params("parallel", "arbitrary"),
        name="short_conv",
    )(u, gb, prev.astype(F32), conv_w.astype(F32))


AUG_ONE0 = 6


def _split3(x):
    hi = x.astype(BF16).astype(F32)
    r = x - hi
    mid = r.astype(BF16).astype(F32)
    lo = (r - mid).astype(BF16).astype(F32)
    return hi, mid, lo


def _cumf_kernel(lfc_ref, lfr_ref, k_ref, fr_ref, kcat_ref):
    t = lfc_ref.shape[0]
    lower = _tri(CHUNK, "row_ge_col")
    upper = _tri(CHUNK, "row_le_col")
    src = lax.broadcasted_iota(jnp.int32, (LANES, LANES), 0)
    dst = lax.broadcasted_iota(jnp.int32, (LANES, LANES), 1)
    lane = lax.broadcasted_iota(jnp.int32, (1, LANES), 1)
    ones = jnp.where((lane >= AUG_ONE0) & (lane < AUG_ONE0 + 3), 1.0, 0.0)
    sel = [[jnp.where(((src == 2 * hp) & (dst == c)) | ((src == 2 * hp + 1) & (dst == 3 + c)), 1.0, 0.0).astype(BF16)
            for c in range(3)] for hp in range(N_PAIRS)]

    def body(c, carry):
        cc, cr = carry
        o = pl.multiple_of(c * CHUNK, CHUNK)
        fc = _dot_f32(lower, lfc_ref[pl.ds(o, CHUNK), :]) + cc
        fr = _dot_f32(lfr_ref[0, :, pl.ds(o, CHUNK)], upper) + cr
        fr_ref[0, :, pl.ds(o, CHUNK)] = fr
        parts = [p.astype(BF16) for p in _split3(fc)]
        for hp in range(N_PAIRS):
            aug = _dot(parts[0], sel[hp][0]) + _dot(parts[1], sel[hp][1]) + _dot(parts[2], sel[hp][2]) + ones
            kcat_ref[pl.ds(o, CHUNK), 2 * hp * LANES:(2 * hp + 1) * LANES] = k_ref[pl.ds(o, CHUNK), hp * LANES:(hp + 1) * LANES]
            kcat_ref[pl.ds(o, CHUNK), (2 * hp + 1) * LANES:(2 * hp + 2) * LANES] = aug.astype(BF16)
        return fc[CHUNK - 1:CHUNK, :], fr[:, CHUNK - 1:CHUNK]

    lax.fori_loop(0, t // CHUNK, body, (jnp.zeros((1, LANES), F32), jnp.zeros((N_HEADS, 1), F32)))


def _cumf(lfc, lfr, kb, b, t):
    return pl.pallas_call(
        _cumf_kernel,
        grid=(b,),
        in_specs=[pl.BlockSpec((t, LANES), lambda i: (i, 0)), pl.BlockSpec((1, N_HEADS, t), lambda i: (i, 0, 0)),
                  pl.BlockSpec((t, D_HALF), lambda i: (i, 0))],
        out_specs=[pl.BlockSpec((1, N_HEADS, t), lambda i: (i, 0, 0)),
                   pl.BlockSpec((t, 2 * D_HALF), lambda i: (i, 0))],
        out_shape=[jax.ShapeDtypeStruct((b, N_HEADS, t), F32),
                   jax.ShapeDtypeStruct((b * t, 2 * D_HALF), BF16)],
        compiler_params=_params("parallel"),
        name="cum_logf",
    )(lfc, lfr, kb)


def _stack_pair_t(qt):
    lo = lax.broadcasted_iota(jnp.int32, (PAIR, 1), 0) < HEAD_DIM
    zero = jnp.zeros_like(qt)
    return jnp.concatenate([jnp.where(lo, qt, zero), jnp.where(lo, zero, qt)], axis=1)


def _unstack_pair_t(acc, tq):
    lo = lax.broadcasted_iota(jnp.int32, (PAIR, 1), 0) < HEAD_DIM
    return jnp.where(lo, acc[:, :tq], acc[:, tq:]).T


def _causal_mask_t(tk, tq, key0, q0, strict):
    kpos = key0 + lax.broadcasted_iota(jnp.int32, (tk, tq), 0)
    qpos = q0 + lax.broadcasted_iota(jnp.int32, (tk, tq), 1)
    ok = (kpos < qpos) if strict else (kpos <= qpos)
    return jnp.concatenate([ok, ok], axis=1)


def _fox_kernel(qt_ref, kcat_ref, vt_ref, fq_ref, o_ref, *, tq, tk, nc):
    i = pl.program_id(2)
    r = lax.broadcasted_iota(jnp.int32, (LANES, tq), 0)

    def qcat_of(c):
        qst = _stack_pair_t(qt_ref[0, c * PAIR:(c + 1) * PAIR, :])
        fq = fq_ref[0, 0, 2 * c:2 * c + 2, :]

        def qaug_half(parity):
            parts = _split3(fq[parity:parity + 1])
            x = jnp.where((r >= 3 * parity) & (r < 3 * parity + 3), -1.0, 0.0)
            for k in range(3):
                x = jnp.where(r == AUG_ONE0 + k, jnp.broadcast_to(parts[k], (LANES, tq)), x)
            return x

        qaug = jnp.concatenate([qaug_half(0), qaug_half(1)], axis=1).astype(BF16)
        return jnp.concatenate([qst, qaug], axis=0)

    qcats = [qcat_of(c) for c in range(nc)]

    def block_one(c, o, carry, masked):
        m, l, acc = carry
        s = _dot(kcat_ref[pl.ds(o, tk), c * 2 * PAIR:(c + 1) * 2 * PAIR], qcats[c])
        if masked:
            s = jnp.where(_causal_mask_t(tk, tq, o, i * tq, False), s, NEG_BIG)
        m_new = jnp.maximum(m, jnp.max(s, axis=0, keepdims=True))
        alpha = jnp.exp(m - m_new)
        p = jnp.exp(s - m_new)
        l = alpha * l + jnp.sum(p, axis=0, keepdims=True)
        acc = alpha * acc + _dot(vt_ref[0, c * PAIR:(c + 1) * PAIR, pl.ds(o, tk)], p.astype(BF16))
        return m_new, l, acc

    def block(j, carries, masked):
        o = pl.multiple_of(j * tk, tk)
        return tuple(block_one(c, o, carries[c], masked) for c in range(nc))

    init = (jnp.full((1, 2 * tq), NEG_BIG, F32), jnp.zeros((1, 2 * tq), F32), jnp.zeros((PAIR, 2 * tq), F32))
    per = tq // tk
    carries = lax.fori_loop(0, i * per, functools.partial(block, masked=False), (init,) * nc)
    for d in range(per):
        carries = block(i * per + d, carries, True)
    for c in range(nc):
        _, l, acc = carries[c]
        o_ref[:, c * PAIR:(c + 1) * PAIR] = _unstack_pair_t(acc / l, tq).astype(BF16)


def _attn_specs(t, tq, nq):
    qt_spec = pl.BlockSpec((1, PAIR, tq), lambda bi, hp, i: (bi, hp, i))
    k_spec = pl.BlockSpec((t, PAIR), lambda bi, hp, i: (bi, hp))
    vt_spec = pl.BlockSpec((1, PAIR, t), lambda bi, hp, i: (bi, hp, 0))
    o_spec = pl.BlockSpec((tq, PAIR), lambda bi, hp, i: (bi * nq + i, hp))
    return qt_spec, k_spec, vt_spec, o_spec


def _fox_attn(qtb, kcat, vtb, frow, b, t):
    tq = tk = min(t, FOX_BLOCK)
    nq = t // tq
    nc = FOX_CHAINS
    return pl.pallas_call(
        functools.partial(_fox_kernel, tq=tq, tk=tk, nc=nc),
        grid=(b, N_PAIRS // nc, nq),
        in_specs=[pl.BlockSpec((1, nc * PAIR, tq), lambda bi, g, i: (bi, g, i)),
                  pl.BlockSpec((t, nc * 2 * PAIR), lambda bi, g, i: (bi, g)),
                  pl.BlockSpec((1, nc * PAIR, t), lambda bi, g, i: (bi, g, 0)),
                  pl.BlockSpec((1, 1, 2 * nc, tq), lambda bi, g, i: (bi, g, 0, i))],
        out_specs=pl.BlockSpec((tq, nc * PAIR), lambda bi, g, i: (bi * nq + i, g)),
        out_shape=jax.ShapeDtypeStruct((b * t, D_HALF), BF16),
        compiler_params=_params("parallel", "parallel", "arbitrary"),
        name="fox_attn",
    )(qtb, kcat, vtb, frow.reshape(b, N_PAIRS // nc, 2 * nc, t))


def _sb_block_t(k, vt, qst, run, acc, tri, mask):
    z = _dot(k, qst)
    lk = -(jnp.maximum(z, 0.0) + jnp.log(1.0 + jnp.exp(-jnp.abs(z))))
    if mask is not None:
        lk = jnp.where(mask, lk, 0.0)
    hi = lk.astype(BF16)
    lo = (lk - hi.astype(F32)).astype(BF16)
    later = _dot(tri, hi) + _dot(tri, lo) + run
    a = jnp.exp(z + lk + later)
    if mask is not None:
        a = jnp.where(mask, a, 0.0)
    acc = acc + _dot(vt, a.astype(BF16))
    run = run + jnp.sum(lk, axis=0, keepdims=True)
    return run, acc


def _sb_kernel(qt_ref, k_ref, vt_ref, o_ref, *, tq, tk):
    i = pl.program_id(2)
    qst = _stack_pair_t(qt_ref[0])
    tri = _tri(tk, "row_lt_col").astype(BF16)
    per = tq // tk
    carry = (jnp.zeros((1, 2 * tq), F32), jnp.zeros((PAIR, 2 * tq), F32))
    for d in reversed(range(per)):
        o = pl.multiple_of((i * per + d) * tk, tk)
        carry = _sb_block_t(k_ref[pl.ds(o, tk), :], vt_ref[0, :, pl.ds(o, tk)], qst, *carry, tri,
                            _causal_mask_t(tk, tq, o, i * tq, True))

    def cond(c):
        n, run, _ = c
        return (n < i * per) & (jnp.max(run) > SB_LOG_FLOOR)

    def body(c):
        n, run, acc = c
        o = pl.multiple_of((i * per - 1 - n) * tk, tk)
        run, acc = _sb_block_t(k_ref[pl.ds(o, tk), :], vt_ref[0, :, pl.ds(o, tk)], qst, run, acc, tri, None)
        return n + 1, run, acc

    _, _, acc = lax.while_loop(cond, body, (jnp.int32(0),) + carry)
    o_ref[...] = _unstack_pair_t(acc, tq).astype(BF16)


def _sb_attn(qtb, kb, vtb, b, t):
    tq = tk = min(t, SB_BLOCK)
    nq = t // tq
    qt_spec, k_spec, vt_spec, o_spec = _attn_specs(t, tq, nq)
    return pl.pallas_call(
        functools.partial(_sb_kernel, tq=tq, tk=tk),
        grid=(b, N_PAIRS, nq),
        in_specs=[qt_spec, k_spec, vt_spec],
        out_specs=o_spec,
        out_shape=jax.ShapeDtypeStruct((b * t, D_HALF), BF16),
        compiler_params=_params("parallel", "parallel", "arbitrary"),
        name="sb_attn",
    )(qtb, kb, vtb)


def _page_specs(block, n):
    nd = len(block) - 1
    return [pl.BlockSpec(block, functools.partial(lambda b, g, pt, p: (pt[b, g * n + p],) + (0,) * nd, p=p))
            for p in range(n)]


def _cumf_dec_kernel(pt_ref, *refs, n):
    pages, lfn_ref, fp_ref, fn_ref, carry_ref = refs[:n], refs[n], refs[n + 1], refs[n + 2], refs[n + 3]
    g = pl.program_id(1)
    upper = _tri(PAGE, "row_le_col")

    @pl.when(g == 0)
    def _():
        carry_ref[...] = jnp.zeros_like(carry_ref)

    stacked = jnp.concatenate([pages[p][0] for p in range(n)] + [lfn_ref[0]], axis=0)
    within = _dot_f32(stacked, upper)
    base = carry_ref[:, 0:1]
    seen = jnp.zeros((N_HEADS, PAGE), F32)
    for p in range(n + 1):
        wp = within[p * N_HEADS:(p + 1) * N_HEADS]
        f = wp + (seen[:, PAGE - 1:PAGE] + base)
        if p < n:
            fp_ref[0, :, p * PAGE:(p + 1) * PAGE] = f
            seen = seen + wp
        else:
            fn_ref[0] = f
    carry_ref[...] = jnp.broadcast_to(seen[:, PAGE - 1:PAGE] + base, carry_ref.shape)


def _cumf_dec(page_table, lf_pool_t, lf_new):
    nb, n_pages = page_table.shape
    n = _row_tile(n_pages, LOGF_PAGES_PER_STEP)
    groups = n_pages // n
    return pl.pallas_call(
        functools.partial(_cumf_dec_kernel, n=n),
        grid_spec=pltpu.PrefetchScalarGridSpec(
            num_scalar_prefetch=1,
            grid=(nb, groups),
            in_specs=_page_specs((1, N_HEADS, PAGE), n) + [pl.BlockSpec((1, N_HEADS, PAGE), lambda b, g, pt: (b, 0, 0))],
            out_specs=[pl.BlockSpec((1, N_HEADS, n * PAGE), lambda b, g, pt: (b, 0, g)),
                       pl.BlockSpec((1, N_HEADS, PAGE), lambda b, g, pt: (b, 0, 0))],
            scratch_shapes=[pltpu.VMEM((N_HEADS, LANES), F32)]),
        out_shape=[jax.ShapeDtypeStruct((nb, N_HEADS, n_pages * PAGE), F32),
                   jax.ShapeDtypeStruct((nb, N_HEADS, PAGE), F32)],
        compiler_params=_params("parallel", "arbitrary"),
        name="cum_logf_paged",
    )(page_table, *([lf_pool_t] * n), lf_new)


def _block_diag_q(q):
    t = q.shape[0]
    rep = jnp.concatenate([q.astype(F32)] * N_HEADS, axis=0)
    rh = lax.broadcasted_iota(jnp.int32, (N_HEADS * t, D_HALF), 0) // t
    lh = lax.broadcasted_iota(jnp.int32, (N_HEADS * t, D_HALF), 1) // HEAD_DIM
    return jnp.where(rh == lh, rep, 0.0).astype(BF16)


def _collapse_heads(acc, t):
    rh = lax.broadcasted_iota(jnp.int32, (N_HEADS * t, D_HALF), 0) // t
    lh = lax.broadcasted_iota(jnp.int32, (N_HEADS * t, D_HALF), 1) // HEAD_DIM
    masked = jnp.where(rh == lh, acc, 0.0)
    out = masked[0:t]
    for h in range(1, N_HEADS):
        out = out + masked[h * t:(h + 1) * t]
    return out


def _rep_heads(x, t):
    return jnp.concatenate([jnp.broadcast_to(x[h:h + 1], (t, x.shape[1])) for h in range(N_HEADS)], axis=0)


def _fox_dec_kernel(pt_ref, *refs, nt):
    n = PAGES_PER_STEP
    kp, vp = refs[:n], refs[n:2 * n]
    q_ref, kn_ref, vn_ref, fp_ref, fn_ref, o_ref, ks_ref, vs_ref, m_ref, l_ref, acc_ref = refs[2 * n:]
    g = pl.program_id(1)
    rows = N_HEADS * nt

    @pl.when(g == 0)
    def _():
        m_ref[...] = jnp.full_like(m_ref, NEG_BIG)
        l_ref[...] = jnp.zeros_like(l_ref)
        acc_ref[...] = jnp.zeros_like(acc_ref)

    for p in range(n):
        ks_ref[:, p * PAGE:(p + 1) * PAGE] = kp[p][0].astype(BF16)
        vs_ref[:, p * PAGE:(p + 1) * PAGE] = vp[p][0].astype(BF16)

    qbd = _block_diag_q(q_ref[0])
    fnew = _rep_heads(fn_ref[0], nt)
    tpos = lax.broadcasted_iota(jnp.int32, (rows, LANES), 0) % nt
    lane = lax.broadcasted_iota(jnp.int32, (rows, LANES), 1)
    fq = jnp.sum(jnp.where(lane == tpos, fnew, 0.0), axis=1, keepdims=True)

    def update(s, vt):
        m_old = m_ref[:, 0:1]
        m_new = jnp.maximum(m_old, jnp.max(s, axis=1, keepdims=True))
        alpha = jnp.exp(m_old - m_new)
        p_ = jnp.exp(s - m_new)
        l_ref[...] = jnp.broadcast_to(alpha * l_ref[:, 0:1] + jnp.sum(p_, axis=1, keepdims=True), l_ref.shape)
        acc_ref[...] = alpha * acc_ref[...] + _dot_nt(p_.astype(BF16), vt)
        m_ref[...] = jnp.broadcast_to(m_new, m_ref.shape)

    s = _dot(qbd, ks_ref[...]) + (fq - _rep_heads(fp_ref[0], nt))
    update(s, vs_ref[...])

    @pl.when(g == pl.num_programs(1) - 1)
    def _():
        sn = _dot(qbd, kn_ref[0]) + (fq - fnew)
        sn = jnp.where(lane <= tpos, sn, NEG_BIG)
        update(sn, vn_ref[0])
        o_ref[0] = _collapse_heads(acc_ref[...] / l_ref[:, 0:1], nt).astype(BF16)


def _fox_dec(page_table, k_pool, v_pool, qb, knt, vnt, f_past, f_new):
    nb, n_pages = page_table.shape
    nt = qb.shape[1]
    n = PAGES_PER_STEP
    groups = n_pages // n
    rows = N_HEADS * nt
    seq = lambda b, g, pt: (b, 0, 0)
    return pl.pallas_call(
        functools.partial(_fox_dec_kernel, nt=nt),
        grid_spec=pltpu.PrefetchScalarGridSpec(
            num_scalar_prefetch=1,
            grid=(nb, groups),
            in_specs=_page_specs((1, D_HALF, PAGE), n) + _page_specs((1, D_HALF, PAGE), n)
            + [pl.BlockSpec((1, nt, D_HALF), seq), pl.BlockSpec((1, D_HALF, PAGE), seq),
               pl.BlockSpec((1, D_HALF, PAGE), seq),
               pl.BlockSpec((1, N_HEADS, n * PAGE), lambda b, g, pt: (b, 0, g)),
               pl.BlockSpec((1, N_HEADS, PAGE), seq)],
            out_specs=pl.BlockSpec((1, nt, D_HALF), seq),
            scratch_shapes=[pltpu.VMEM((D_HALF, n * PAGE), BF16), pltpu.VMEM((D_HALF, n * PAGE), BF16),
                            pltpu.VMEM((rows, LANES), F32), pltpu.VMEM((rows, LANES), F32),
                            pltpu.VMEM((rows, D_HALF), F32)]),
        out_shape=jax.ShapeDtypeStruct((nb, nt, D_HALF), BF16),
        compiler_params=_params("parallel", "arbitrary"),
        name="fox_decode",
    )(page_table, *([k_pool] * n), *([v_pool] * n), qb, knt, vnt, f_past, f_new)


def _sb_block(qs, kt, vt, run, acc, strict_tri, mask):
    z = _dot(qs, kt)
    lk = -(jnp.maximum(z, 0.0) + jnp.log(1.0 + jnp.exp(-jnp.abs(z))))
    if mask is not None:
        lk = jnp.where(mask, lk, 0.0)
    hi = lk.astype(BF16)
    lo = (lk - hi.astype(F32)).astype(BF16)
    later = _dot(hi, strict_tri) + _dot(lo, strict_tri) + run
    a = jnp.exp(z + lk + later)
    if mask is not None:
        a = jnp.where(mask, a, 0.0)
    acc = acc + _dot_nt(a.astype(BF16), vt)
    run = run + jnp.sum(lk, axis=1, keepdims=True)
    return run, acc


def _sb_dec_kernel(pt_ref, q_ref, kn_ref, vn_ref, kpool_ref, vpool_ref, o_ref, kbuf, vbuf, sem, *, nt, n_pages, gp):
    b = pl.program_id(0)
    slot = b % 2
    rows = N_HEADS * nt
    n_groups = n_pages // gp
    tri = _tri(PAGE, "row_gt_col").astype(BF16)
    qbd = _block_diag_q(q_ref[0])

    def group_copies(seq, grp, slot_):
        out = []
        for p in range(gp):
            page = pt_ref[seq, n_pages - (grp + 1) * gp + p]
            out.append(pltpu.make_async_copy(kpool_ref.at[page], kbuf.at[slot_, p], sem.at[slot_, 0]))
            out.append(pltpu.make_async_copy(vpool_ref.at[page], vbuf.at[slot_, p], sem.at[slot_, 1]))
        return out

    def visit_group(slot_, run, acc):
        for p in reversed(range(gp)):
            run, acc = _sb_block(qbd, kbuf[slot_, p].astype(BF16), vbuf[slot_, p].astype(BF16), run, acc, tri, None)
        return run, acc

    @pl.when(b == 0)
    def _():
        for c in group_copies(0, 0, 0):
            c.start()

    @pl.when(b + 1 < pl.num_programs(0))
    def _():
        for c in group_copies(b + 1, 0, 1 - slot):
            c.start()

    tpos = lax.broadcasted_iota(jnp.int32, (rows, LANES), 0) % nt
    lane = lax.broadcasted_iota(jnp.int32, (rows, LANES), 1)
    run, acc = _sb_block(qbd, kn_ref[0], vn_ref[0], jnp.zeros((rows, 1), F32), jnp.zeros((rows, D_HALF), F32),
                         tri, lane < tpos)

    for c in group_copies(b, 0, slot):
        c.wait()
    run, acc = visit_group(slot, run, acc)

    def cond(c):
        g, run, _ = c
        return (g < n_groups) & (jnp.max(run) > SB_LOG_FLOOR)

    def body(c):
        g, run, acc = c
        copies = group_copies(b, g, slot)
        for cp in copies:
            cp.start()
        for cp in copies:
            cp.wait()
        run, acc = visit_group(slot, run, acc)
        return g + 1, run, acc

    _, _, acc = lax.while_loop(cond, body, (jnp.int32(1), run, acc))
    o_ref[0] = _collapse_heads(acc, nt).astype(BF16)


def _sb_dec(page_table, k_pool, v_pool, qb, knt, vnt):
    nb, n_pages = page_table.shape
    nt = qb.shape[1]
    gp = _row_tile(n_pages, SB_PAGES_PER_FETCH)
    seq = lambda b, pt: (b, 0, 0)
    return pl.pallas_call(
        functools.partial(_sb_dec_kernel, nt=nt, n_pages=n_pages, gp=gp),
        grid_spec=pltpu.PrefetchScalarGridSpec(
            num_scalar_prefetch=1,
            grid=(nb,),
            in_specs=[pl.BlockSpec((1, nt, D_HALF), seq), pl.BlockSpec((1, D_HALF, PAGE), seq),
                      pl.BlockSpec((1, D_HALF, PAGE), seq),
                      pl.BlockSpec(memory_space=pl.ANY), pl.BlockSpec(memory_space=pl.ANY)],
            out_specs=pl.BlockSpec((1, nt, D_HALF), seq),
            scratch_shapes=[pltpu.VMEM((2, gp, D_HALF, PAGE), F32), pltpu.VMEM((2, gp, D_HALF, PAGE), F32),
                            pltpu.SemaphoreType.DMA((2, 2))]),
        out_shape=jax.ShapeDtypeStruct((nb, nt, D_HALF), BF16),
        compiler_params=_params("arbitrary"),
        name="sb_decode",
    )(page_table, qb, knt, vnt, k_pool, v_pool)


def _new_feat_major(kb, b, t):
    return jnp.pad(kb.reshape(b, t, D_HALF).transpose(0, 2, 1), ((0, 0), (0, 0), (0, PAGE - t)))


def _heads_last(kt, b, t):
    return kt.reshape(b, N_HEADS, HEAD_DIM, t).transpose(0, 3, 1, 2)[None]


def _trunk(x, p, cache):
    b, t, d = x.shape
    m = b * t
    x2 = x.reshape(m, d)
    prompt = cache is None

    proj = _proj_even(x2, p["g_mix"][0], p["w_in_even"][0], p["b_forget"][0], p["sgu_g"][0], p["sgu_b"][0],
                      b, t, prompt)
    lfc, lfr, u, vn, vnb = proj[5:]
    w_s, b_s = p["w_spatial"][0].astype(F32), p["b_spatial"][0].astype(F32)
    if prompt:
        qt, kb, kt, vt, vtb = proj[:5]
        frow, kcat = _cumf(lfc, lfr, kb, b, t)
        a = _fox_attn(qt, kcat, vtb, frow, b, t)
        mask = np.tril(np.ones((CHUNK, CHUNK), np.float32))
        bias = jnp.repeat(b_s.T, HEAD_DIM, axis=1)
        w_mix = w_s
        even_rows = (_heads_last(kt, b, t), _heads_last(vt, b, t), lfr.transpose(0, 2, 1)[None], None)
    else:
        qb, k, kb, v, vb = proj[:5]
        pt = cache["page_table"]
        lf_new = jnp.pad(lfr.reshape(N_HEADS, b, t).transpose(1, 0, 2), ((0, 0), (0, 0), (0, PAGE - t)))
        f_past, f_new = _cumf_dec(pt, cache["fox_logf_t"], lf_new)
        a = _fox_dec(pt, cache["fox_k"], cache["fox_v"], qb.reshape(b, t, D_HALF),
                     _new_feat_major(kb, b, t), _new_feat_major(vb, b, t), f_past, f_new).reshape(m, D_HALF)
        reps = CHUNK // t
        idx = np.arange(CHUNK)
        mask = ((idx[:, None] // t == idx[None, :] // t) & (idx[None, :] % t <= idx[:, None] % t)).astype(np.float32)
        w_mix = jnp.tile(w_s[:, :t, :t], (1, reps, reps))
        bias = jnp.tile(jnp.repeat(b_s.T[:t], HEAD_DIM, axis=1), (reps, 1))
        hd = lambda z: z.reshape(1, b, t, N_HEADS, HEAD_DIM)
        even_rows = (hd(k), hd(v), lfc[:, :N_HEADS].reshape(1, b, t, N_HEADS), vn.reshape(1, b, t, D_HALF))
    ob = _sgu(vnb, u, w_mix, jnp.asarray(mask), bias)
    x2 = _mix_mlp(x2, a, ob, p["w_out_even"][0], p["g_mlp"][0], p["w_up"][0], p["w_down"][0], p["g_final"], False)

    proj = _proj_odd(x2, p["g_mix"][1], p["w_in_odd"][0], b, t, prompt)
    gb, uc = proj[5:]
    if prompt:
        qt, kb, kt, vt, vtb = proj[:5]
        a = _sb_attn(qt, kb, vtb, b, t)
        prev = jnp.zeros((b, CONV_W - 1, D_HALF), F32)
        kv_rows = (_heads_last(kt, b, t), _heads_last(vt, b, t))
    else:
        qb, k, kb, v, vb = proj[:5]
        a = _sb_dec(cache["page_table"], cache["sb_k"], cache["sb_v"], qb.reshape(b, t, D_HALF),
                    _new_feat_major(kb, b, t), _new_feat_major(vb, b, t)).reshape(m, D_HALF)
        prev = cache["conv"]
        kv_rows = (k.reshape(1, b, t, N_HEADS, HEAD_DIM), v.reshape(1, b, t, N_HEADS, HEAD_DIM))
    od, new_conv = _conv(uc.reshape(b, t, D_HALF), gb.reshape(b, t, D_HALF), prev, p["conv_w"][0])
    y = _mix_mlp(x2, a, od.reshape(m, D_HALF), p["w_out_odd"][0], p["g_mlp"][1], p["w_up"][1], p["w_down"][1],
                 p["g_final"], True)
    return y.reshape(b, t, d), even_rows, kv_rows + (new_conv[None],)


def _pool_feat_major(cache):
    n_pool = cache.shape[1]
    return cache[0].transpose(0, 2, 3, 1).reshape(n_pool, D_HALF, PAGE)


def kernel(x_prompt, x_sample, cache_fox_k, cache_fox_v, cache_fox_logf, cache_sb_k, cache_sb_v, state_conv,
           page_table, g_mix, g_mlp, g_final, w_up, w_down, w_in_even, b_forget, sgu_g, sgu_b, w_spatial,
           b_spatial, w_out_even, w_in_odd, conv_w, w_out_odd):
    p = dict(g_mix=g_mix, g_mlp=g_mlp, g_final=g_final, w_up=w_up, w_down=w_down, w_in_even=w_in_even,
             b_forget=b_forget, sgu_g=sgu_g, sgu_b=sgu_b, w_spatial=w_spatial, b_spatial=b_spatial,
             w_out_even=w_out_even, w_in_odd=w_in_odd, conv_w=conv_w, w_out_odd=w_out_odd)
    y_p, (pk, pv, plf, _), (psk, psv, pconv) = _trunk(x_prompt, p, None)
    cache = dict(page_table=page_table,
                 fox_k=_pool_feat_major(cache_fox_k), fox_v=_pool_feat_major(cache_fox_v),
                 fox_logf_t=cache_fox_logf[0].transpose(0, 2, 1),
                 sb_k=_pool_feat_major(cache_sb_k), sb_v=_pool_feat_major(cache_sb_v),
                 conv=state_conv[0])
    y_s, (sk, sv, slf, svn), (ssk, ssv, sconv) = _trunk(x_sample, p, cache)
    return (y_p, y_s, pk, pv, plf, psk, psv, pconv, sk, sv, slf, ssk, ssv, sconv, svn)
```

```python
import functools

import numpy as np
import jax
import jax.numpy as jnp
from jax import lax
from jax.experimental import pallas as pl
from jax.experimental.pallas import tpu as pltpu

F32 = jnp.float32
BF16 = jnp.bfloat16

HEAD_DIM = 64
N_HEADS = 8
PAIR = 2 * HEAD_DIM
N_PAIRS = N_HEADS // 2
D_HALF = N_HEADS * HEAD_DIM
CHUNK = 128
PAGE = 128
CONV_W = 3
EPS = 1e-6
QK_SCALE = HEAD_DIM ** -0.5
NEG_BIG = -1e30
LANES = 128
VMEM_LIMIT = 56 * 1024 * 1024
PAGES_PER_STEP = 32
LOGF_PAGES_PER_STEP = 64
SB_PAGES_PER_FETCH = 2
FOX_BLOCK = 1024
FOX_CHAINS = 2
SB_BLOCK = 256
SB_LOG_FLOOR = -110.0

NT_DIMS = (((1,), (1,)), ((), ()))


def _dot(a, b):
    return jnp.dot(a, b, preferred_element_type=F32)


def _dot_nt(a, b):
    return lax.dot_general(a, b, NT_DIMS, preferred_element_type=F32)


def _dot_f32(a, b):
    return jnp.dot(a, b, preferred_element_type=F32, precision=lax.Precision.HIGHEST)


def _rms(x, g):
    return x * lax.rsqrt(jnp.mean(x * x, axis=-1, keepdims=True) + EPS) * g


def _log_sigmoid(x):
    return jnp.minimum(x, 0.0) - jnp.log1p(jnp.exp(-jnp.abs(x)))


def _params(*sem):
    return pltpu.CompilerParams(dimension_semantics=sem, vmem_limit_bytes=VMEM_LIMIT)


def _resident(shape):
    nd = len(shape)
    return pl.BlockSpec(shape, lambda *_: (0,) * nd, pipeline_mode=pl.Buffered(1))


def _row_tile(m, cap):
    t = min(m, cap)
    assert m % t == 0
    return t


def _tri(n, kind):
    r = lax.broadcasted_iota(jnp.int32, (n, n), 0)
    c = lax.broadcasted_iota(jnp.int32, (n, n), 1)
    keep = {"row_le_col": r <= c, "row_ge_col": r >= c, "row_gt_col": r > c, "row_lt_col": r < c}[kind]
    return jnp.where(keep, 1.0, 0.0).astype(F32)


def _emit_qkv(hb, w_refs, o_refs, prompt):
    if prompt:
        wqt, wk, wkt, wvt = w_refs
        qt_ref, kb_ref, kt_ref, vt_ref, vtb_ref = o_refs
        qt_ref[0] = (_dot_nt(wqt[...], hb) * QK_SCALE).astype(BF16)
        kb_ref[...] = _dot(hb, wk[...]).astype(BF16)
        kt_ref[0] = _dot_nt(wkt[...], hb)
        vt = _dot_nt(wvt[...], hb)
        vt_ref[0] = vt
        vtb_ref[0] = vt.astype(BF16)
    else:
        wq, wk, wv = w_refs
        q_ref, k_ref, kb_ref, v_ref, vb_ref = o_refs
        q_ref[...] = (_dot(hb, wq[...]) * QK_SCALE).astype(BF16)
        k = _dot(hb, wk[...])
        k_ref[...] = k
        kb_ref[...] = k.astype(BF16)
        v = _dot(hb, wv[...])
        v_ref[...] = v
        vb_ref[...] = v.astype(BF16)


def _qkv_weights(wq, wk, wv, prompt):
    return [wq.T, wk, wk.T, wv.T] if prompt else [wq, wk, wv]


def _tok_out(m, tm, w, dt):
    return jax.ShapeDtypeStruct((m, w), dt), pl.BlockSpec((tm, w), lambda i: (i, 0))


def _feat_out(b, t, tm, w, dt):
    nt = t // tm
    return jax.ShapeDtypeStruct((b, w, t), dt), pl.BlockSpec((1, w, tm), lambda i: (i // nt, 0, i % nt))


def _qkv_outs(b, t, tm, prompt):
    m = b * t
    if prompt:
        return [_feat_out(b, t, tm, D_HALF, BF16), _tok_out(m, tm, D_HALF, BF16), _feat_out(b, t, tm, D_HALF, F32),
                _feat_out(b, t, tm, D_HALF, F32), _feat_out(b, t, tm, D_HALF, BF16)]
    return [_tok_out(m, tm, D_HALF, BF16), _tok_out(m, tm, D_HALF, F32), _tok_out(m, tm, D_HALF, BF16),
            _tok_out(m, tm, D_HALF, F32), _tok_out(m, tm, D_HALF, BF16)]


def _proj_even_kernel(*refs, prompt):
    nw = 4 if prompt else 3
    x_ref, g_ref = refs[:2]
    w_refs = refs[2:2 + nw]
    wf_ref, wft_ref, wu_ref, wg_ref, bfc_ref, bfr_ref, sg_ref, sb_ref = refs[2 + nw:10 + nw]
    o_refs = refs[10 + nw:15 + nw]
    lfc_ref, lfr_ref, u_ref, vn_ref, vnb_ref = refs[15 + nw:]
    hb = _rms(x_ref[...], g_ref[...]).astype(BF16)
    _emit_qkv(hb, w_refs, o_refs, prompt)
    lfc_ref[...] = _log_sigmoid(_dot(hb, wf_ref[...]) + bfc_ref[...])
    lfr = _log_sigmoid(_dot_nt(wft_ref[...], hb)[:N_HEADS] + bfr_ref[...])
    if prompt:
        lfr_ref[0] = lfr
    else:
        lfr_ref[...] = lfr
    u_ref[...] = _dot(hb, wu_ref[...])
    vg = _dot(hb, wg_ref[...])
    mu = jnp.mean(vg, axis=-1, keepdims=True)
    vc = vg - mu
    var = jnp.mean(vc * vc, axis=-1, keepdims=True)
    vn = vc * lax.rsqrt(var + EPS) * sg_ref[...] + sb_ref[...]
    vn_ref[...] = vn
    vnb_ref[...] = vn.astype(BF16)


def _proj_even(x, g, w_in, b_forget, sgu_g, sgu_b, b, t, prompt):
    m, d = x.shape
    tm = _row_tile(t if prompt else m, 512)
    wb = w_in.astype(BF16)
    o = 3 * D_HALF
    qkv_w = _qkv_weights(wb[:, :D_HALF], wb[:, D_HALF:2 * D_HALF], wb[:, 2 * D_HALF:o], prompt)
    wf = jnp.pad(wb[:, o:o + N_HEADS], ((0, 0), (0, LANES - N_HEADS)))
    wft = jnp.pad(wb[:, o:o + N_HEADS].T, ((0, 16 - N_HEADS), (0, 0)))
    wu, wg = wb[:, o + N_HEADS:o + N_HEADS + D_HALF], wb[:, o + N_HEADS + D_HALF:]
    bfc = jnp.pad(b_forget.astype(F32), (0, LANES - N_HEADS)).reshape(1, LANES)
    bfr = b_forget.astype(F32).reshape(N_HEADS, 1)
    lfr_out = (_feat_out(b, t, tm, N_HEADS, F32) if prompt else
               (jax.ShapeDtypeStruct((N_HEADS, m), F32), pl.BlockSpec((N_HEADS, tm), lambda i: (0, i))))
    outs = (_qkv_outs(b, t, tm, prompt)
            + [_tok_out(m, tm, LANES, F32), lfr_out,
               _tok_out(m, tm, D_HALF, F32), _tok_out(m, tm, D_HALF, F32), _tok_out(m, tm, D_HALF, BF16)])
    weights = qkv_w + [wf, wft, wu, wg, bfc, bfr,
                       sgu_g.astype(F32).reshape(1, D_HALF), sgu_b.astype(F32).reshape(1, D_HALF)]
    return pl.pallas_call(
        functools.partial(_proj_even_kernel, prompt=prompt),
        grid=(m // tm,),
        in_specs=[pl.BlockSpec((tm, d), lambda i: (i, 0)), _resident((1, d))] + [_resident(w.shape) for w in weights],
        out_specs=[s for _, s in outs],
        out_shape=[s for s, _ in outs],
        compiler_params=_params("parallel"),
        name="proj_even",
    )(x, g.astype(F32).reshape(1, d), *weights)


def _proj_odd_kernel(*refs, prompt):
    nw = 4 if prompt else 3
    x_ref, g_ref = refs[:2]
    w_refs = refs[2:2 + nw]
    wgb_ref, wgc_ref, wh_ref = refs[2 + nw:5 + nw]
    o_refs = refs[5 + nw:10 + nw]
    gb_ref, u_ref = refs[10 + nw:]
    hb = _rms(x_ref[...], g_ref[...]).astype(BF16)
    _emit_qkv(hb, w_refs, o_refs, prompt)
    gb_ref[...] = _dot(hb, wgb_ref[...])
    u_ref[...] = _dot(hb, wgc_ref[...]) * _dot(hb, wh_ref[...])


def _proj_odd(x, g, w_in, b, t, prompt):
    m, d = x.shape
    tm = _row_tile(t if prompt else m, 512)
    wb = w_in.astype(BF16)
    ws = [wb[:, i * D_HALF:(i + 1) * D_HALF] for i in range(6)]
    weights = _qkv_weights(ws[0], ws[1], ws[2], prompt) + ws[3:]
    outs = _qkv_outs(b, t, tm, prompt) + [_tok_out(m, tm, D_HALF, F32), _tok_out(m, tm, D_HALF, F32)]
    return pl.pallas_call(
        functools.partial(_proj_odd_kernel, prompt=prompt),
        grid=(m // tm,),
        in_specs=[pl.BlockSpec((tm, d), lambda i: (i, 0)), _resident((1, d))] + [_resident(w.shape) for w in weights],
        out_specs=[s for _, s in outs],
        out_shape=[s for s, _ in outs],
        compiler_params=_params("parallel"),
        name="proj_odd",
    )(x, g.astype(F32).reshape(1, d), *weights)


def _mix_mlp_kernel(x_ref, a_ref, b_ref, woa_ref, wob_ref, g_ref, wup_ref, wdn_ref, gf_ref, o_ref,
                    *, final_norm, ff_chunk):
    x1 = x_ref[...] + _dot(a_ref[...], woa_ref[...]) + _dot(b_ref[...], wob_ref[...])
    hb = _rms(x1, g_ref[...]).astype(BF16)
    y = None
    for c in range(wup_ref.shape[1] // ff_chunk):
        sl = slice(c * ff_chunk, (c + 1) * ff_chunk)
        up = jnp.maximum(_dot(hb, wup_ref[:, sl]), 0.0)
        down = _dot((up * up).astype(BF16), wdn_ref[sl, :])
        y = down if y is None else y + down
    out = x1 + y
    o_ref[...] = _rms(out, gf_ref[...]) if final_norm else out


def _mix_mlp(x, a, b, w_out, g_mlp, w_up, w_down, g_final, final_norm):
    m, d = x.shape
    tm = _row_tile(m, 512)
    wo = w_out.astype(BF16)
    weights = [wo[:D_HALF], wo[D_HALF:], g_mlp.astype(F32).reshape(1, d), w_up.astype(BF16),
               w_down.astype(BF16), g_final.astype(F32).reshape(1, d)]
    row = lambda i: (i, 0)
    return pl.pallas_call(
        functools.partial(_mix_mlp_kernel, final_norm=final_norm, ff_chunk=1024),
        grid=(m // tm,),
        in_specs=[pl.BlockSpec((tm, d), row), pl.BlockSpec((tm, D_HALF), row), pl.BlockSpec((tm, D_HALF), row)]
        + [_resident(w.shape) for w in weights],
        out_specs=pl.BlockSpec((tm, d), row),
        out_shape=jax.ShapeDtypeStruct((m, d), F32),
        compiler_params=_params("parallel"),
        name="mix_mlp",
    )(x, a, b, *weights)


def _sgu_kernel(vn_ref, u_ref, w_ref, mask_ref, bias_ref, o_ref):
    lo = lax.broadcasted_iota(jnp.int32, (1, PAIR), 1) < HEAD_DIM
    keep = mask_ref[...] > 0.0
    for gp in range(N_PAIRS):
        we = jnp.where(keep, w_ref[2 * gp], 0.0).astype(BF16)
        wo = jnp.where(keep, w_ref[2 * gp + 1], 0.0).astype(BF16)
        cols = slice(gp * PAIR, (gp + 1) * PAIR)
        for c in range(vn_ref.shape[0] // CHUNK):
            rows = slice(c * CHUNK, (c + 1) * CHUNK)
            vn2 = vn_ref[rows, cols]
            mixed = jnp.where(lo, _dot(we, vn2), _dot(wo, vn2)) + bias_ref[:, cols]
            o_ref[rows, cols] = (u_ref[rows, cols] * mixed).astype(BF16)


def _sgu(vnb, u, w, mask, bias):
    m = vnb.shape[0]
    assert m % CHUNK == 0
    tm = _row_tile(m, 512)
    row = lambda i: (i, 0)
    return pl.pallas_call(
        _sgu_kernel,
        grid=(m // tm,),
        in_specs=[pl.BlockSpec((tm, D_HALF), row), pl.BlockSpec((tm, D_HALF), row),
                  _resident(w.shape), _resident(mask.shape), _resident(bias.shape)],
        out_specs=pl.BlockSpec((tm, D_HALF), row),
        out_shape=jax.ShapeDtypeStruct((m, D_HALF), BF16),
        compiler_params=_params("parallel"),
        name="sgu",
    )(vnb, u, w, mask, bias)


def _conv_kernel(u_ref, gb_ref, prev_ref, w_ref, o_ref, new_ref, carry_ref):
    i = pl.program_id(1)

    @pl.when(i == 0)
    def _():
        carry_ref[...] = jnp.zeros_like(carry_ref)
        carry_ref[6:8, :] = prev_ref[0]

    u = u_ref[0]
    tt = u.shape[0]
    prev = carry_ref[...]
    r = lax.broadcasted_iota(jnp.int32, (tt, 1), 0)
    u1 = pltpu.roll(u, 1, 0)
    u2 = pltpu.roll(u, 2, 0)
    p1 = jnp.broadcast_to(prev[7:8], u.shape)
    p2 = jnp.where(r == 0, jnp.broadcast_to(prev[6:7], u.shape), p1)
    u1 = jnp.where(r == 0, p1, u1)
    u2 = jnp.where(r <= 1, p2, u2)
    conv = w_ref[0:1] * u2 + w_ref[1:2] * u1 + w_ref[2:3] * u
    o_ref[0] = (gb_ref[0] * conv).astype(BF16)
    if tt >= 8:
        tail = u[tt - 8:tt]
    else:
        tail = jnp.concatenate([prev[tt:8], u], axis=0)
    carry_ref[...] = tail
    new_ref[0] = tail[6:8]


def _conv(u, gb, prev, conv_w):
    b, t, d = u.shape
    tt = _row_tile(t, 1024)
    blk = lambda bi, i: (bi, i, 0)
    return pl.pallas_call(
        _conv_kernel,
        grid=(b, t // tt),
        in_specs=[pl.BlockSpec((1, tt, d), blk), pl.BlockSpec((1, tt, d), blk),
                  pl.BlockSpec((1, CONV_W - 1, d), lambda bi, i: (bi, 0, 0)), _resident((CONV_W, d))],
        out_specs=[pl.BlockSpec((1, tt, d), blk), pl.BlockSpec((1, CONV_W - 1, d), lambda bi, i: (bi, 0, 0))],
        out_shape=[jax.ShapeDtypeStruct((b, t, d), BF16), jax.ShapeDtypeStruct((b, CONV_W - 1, d), F32)],
        scratch_shapes=[pltpu.VMEM((8, d), F32)],
        compiler_params=_params("parallel", "arbitrary"),
        name="short_conv",
    )(u, gb, prev.astype(F32), conv_w.astype(F32))


AUG_ONE0 = 6


def _split3(x):
    hi = x.astype(BF16).astype(F32)
    r = x - hi
    mid = r.astype(BF16).astype(F32)
    lo = (r - mid).astype(BF16).astype(F32)
    return hi, mid, lo


def _cumf_kernel(lfc_ref, lfr_ref, k_ref, fr_ref, kcat_ref):
    t = lfc_ref.shape[0]
    lower = _tri(CHUNK, "row_ge_col")
    upper = _tri(CHUNK, "row_le_col")
    src = lax.broadcasted_iota(jnp.int32, (LANES, LANES), 0)
    dst = lax.broadcasted_iota(jnp.int32, (LANES, LANES), 1)
    lane = lax.broadcasted_iota(jnp.int32, (1, LANES), 1)
    ones = jnp.where((lane >= AUG_ONE0) & (lane < AUG_ONE0 + 3), 1.0, 0.0)
    sel = [[jnp.where(((src == 2 * hp) & (dst == c)) | ((src == 2 * hp + 1) & (dst == 3 + c)), 1.0, 0.0).astype(BF16)
            for c in range(3)] for hp in range(N_PAIRS)]

    def body(c, carry):
        cc, cr = carry
        o = pl.multiple_of(c * CHUNK, CHUNK)
        fc = _dot_f32(lower, lfc_ref[pl.ds(o, CHUNK), :]) + cc
        fr = _dot_f32(lfr_ref[0, :, pl.ds(o, CHUNK)], upper) + cr
        fr_ref[0, :, pl.ds(o, CHUNK)] = fr
        parts = [p.astype(BF16) for p in _split3(fc)]
        for hp in range(N_PAIRS):
            aug = _dot(parts[0], sel[hp][0]) + _dot(parts[1], sel[hp][1]) + _dot(parts[2], sel[hp][2]) + ones
            kcat_ref[pl.ds(o, CHUNK), 2 * hp * LANES:(2 * hp + 1) * LANES] = k_ref[pl.ds(o, CHUNK), hp * LANES:(hp + 1) * LANES]
            kcat_ref[pl.ds(o, CHUNK), (2 * hp + 1) * LANES:(2 * hp + 2) * LANES] = aug.astype(BF16)
        return fc[CHUNK - 1:CHUNK, :], fr[:, CHUNK - 1:CHUNK]

    lax.fori_loop(0, t // CHUNK, body, (jnp.zeros((1, LANES), F32), jnp.zeros((N_HEADS, 1), F32)))


def _cumf(lfc, lfr, kb, b, t):
    return pl.pallas_call(
        _cumf_kernel,
        grid=(b,),
        in_specs=[pl.BlockSpec((t, LANES), lambda i: (i, 0)), pl.BlockSpec((1, N_HEADS, t), lambda i: (i, 0, 0)),
                  pl.BlockSpec((t, D_HALF), lambda i: (i, 0))],
        out_specs=[pl.BlockSpec((1, N_HEADS, t), lambda i: (i, 0, 0)),
                   pl.BlockSpec((t, 2 * D_HALF), lambda i: (i, 0))],
        out_shape=[jax.ShapeDtypeStruct((b, N_HEADS, t), F32),
                   jax.ShapeDtypeStruct((b * t, 2 * D_HALF), BF16)],
        compiler_params=_params("parallel"),
        name="cum_logf",
    )(lfc, lfr, kb)


def _stack_pair_t(qt):
    lo = lax.broadcasted_iota(jnp.int32, (PAIR, 1), 0) < HEAD_DIM
    zero = jnp.zeros_like(qt)
    return jnp.concatenate([jnp.where(lo, qt, zero), jnp.where(lo, zero, qt)], axis=1)


def _unstack_pair_t(acc, tq):
    lo = lax.broadcasted_iota(jnp.int32, (PAIR, 1), 0) < HEAD_DIM
    return jnp.where(lo, acc[:, :tq], acc[:, tq:]).T


def _causal_mask_t(tk, tq, key0, q0, strict):
    kpos = key0 + lax.broadcasted_iota(jnp.int32, (tk, tq), 0)
    qpos = q0 + lax.broadcasted_iota(jnp.int32, (tk, tq), 1)
    ok = (kpos < qpos) if strict else (kpos <= qpos)
    return jnp.concatenate([ok, ok], axis=1)


def _fox_kernel(qt_ref, kcat_ref, vt_ref, fq_ref, o_ref, *, tq, tk, nc):
    i = pl.program_id(2)
    r = lax.broadcasted_iota(jnp.int32, (LANES, tq), 0)

    def qcat_of(c):
        qst = _stack_pair_t(qt_ref[0, c * PAIR:(c + 1) * PAIR, :])
        fq = fq_ref[0, 0, 2 * c:2 * c + 2, :]

        def qaug_half(parity):
            parts = _split3(fq[parity:parity + 1])
            x = jnp.where((r >= 3 * parity) & (r < 3 * parity + 3), -1.0, 0.0)
            for k in range(3):
                x = jnp.where(r == AUG_ONE0 + k, jnp.broadcast_to(parts[k], (LANES, tq)), x)
            return x

        qaug = jnp.concatenate([qaug_half(0), qaug_half(1)], axis=1).astype(BF16)
        return jnp.concatenate([qst, qaug], axis=0)

    qcats = [qcat_of(c) for c in range(nc)]

    def block_one(c, o, carry, masked):
        m, l, acc = carry
        s = _dot(kcat_ref[pl.ds(o, tk), c * 2 * PAIR:(c + 1) * 2 * PAIR], qcats[c])
        if masked:
            s = jnp.where(_causal_mask_t(tk, tq, o, i * tq, False), s, NEG_BIG)
        m_new = jnp.maximum(m, jnp.max(s, axis=0, keepdims=True))
        alpha = jnp.exp(m - m_new)
        p = jnp.exp(s - m_new)
        l = alpha * l + jnp.sum(p, axis=0, keepdims=True)
        acc = alpha * acc + _dot(vt_ref[0, c * PAIR:(c + 1) * PAIR, pl.ds(o, tk)], p.astype(BF16))
        return m_new, l, acc

    def block(j, carries, masked):
        o = pl.multiple_of(j * tk, tk)
        return tuple(block_one(c, o, carries[c], masked) for c in range(nc))

    init = (jnp.full((1, 2 * tq), NEG_BIG, F32), jnp.zeros((1, 2 * tq), F32), jnp.zeros((PAIR, 2 * tq), F32))
    per = tq // tk
    carries = lax.fori_loop(0, i * per, functools.partial(block, masked=False), (init,) * nc)
    for d in range(per):
        carries = block(i * per + d, carries, True)
    for c in range(nc):
        _, l, acc = carries[c]
        o_ref[:, c * PAIR:(c + 1) * PAIR] = _unstack_pair_t(acc / l, tq).astype(BF16)


def _attn_specs(t, tq, nq):
    qt_spec = pl.BlockSpec((1, PAIR, tq), lambda bi, hp, i: (bi, hp, i))
    k_spec = pl.BlockSpec((t, PAIR), lambda bi, hp, i: (bi, hp))
    vt_spec = pl.BlockSpec((1, PAIR, t), lambda bi, hp, i: (bi, hp, 0))
    o_spec = pl.BlockSpec((tq, PAIR), lambda bi, hp, i: (bi * nq + i, hp))
    return qt_spec, k_spec, vt_spec, o_spec


def _fox_attn(qtb, kcat, vtb, frow, b, t):
    tq = tk = min(t, FOX_BLOCK)
    nq = t // tq
    nc = FOX_CHAINS
    return pl.pallas_call(
        functools.partial(_fox_kernel, tq=tq, tk=tk, nc=nc),
        grid=(b, N_PAIRS // nc, nq),
        in_specs=[pl.BlockSpec((1, nc * PAIR, tq), lambda bi, g, i: (bi, g, i)),
                  pl.BlockSpec((t, nc * 2 * PAIR), lambda bi, g, i: (bi, g)),
                  pl.BlockSpec((1, nc * PAIR, t), lambda bi, g, i: (bi, g, 0)),
                  pl.BlockSpec((1, 1, 2 * nc, tq), lambda bi, g, i: (bi, g, 0, i))],
        out_specs=pl.BlockSpec((tq, nc * PAIR), lambda bi, g, i: (bi * nq + i, g)),
        out_shape=jax.ShapeDtypeStruct((b * t, D_HALF), BF16),
        compiler_params=_params("parallel", "parallel", "arbitrary"),
        name="fox_attn",
    )(qtb, kcat, vtb, frow.reshape(b, N_PAIRS // nc, 2 * nc, t))


def _sb_block_t(k, vt, qst, run, acc, tri, mask):
    z = _dot(k, qst)
    lk = -(jnp.maximum(z, 0.0) + jnp.log(1.0 + jnp.exp(-jnp.abs(z))))
    if mask is not None:
        lk = jnp.where(mask, lk, 0.0)
    hi = lk.astype(BF16)
    lo = (lk - hi.astype(F32)).astype(BF16)
    later = _dot(tri, hi) + _dot(tri, lo) + run
    a = jnp.exp(z + lk + later)
    if mask is not None:
        a = jnp.where(mask, a, 0.0)
    acc = acc + _dot(vt, a.astype(BF16))
    run = run + jnp.sum(lk, axis=0, keepdims=True)
    return run, acc


def _sb_kernel(qt_ref, k_ref, vt_ref, o_ref, *, tq, tk):
    i = pl.program_id(2)
    qst = _stack_pair_t(qt_ref[0])
    tri = _tri(tk, "row_lt_col").astype(BF16)
    first = min(2 * tq, k_ref.shape[0])
    tri_first = _tri(first, "row_lt_col").astype(BF16)
    o = pl.multiple_of(jnp.maximum(i - 1, 0) * tq, tq)
    carry = _sb_block_t(k_ref[pl.ds(o, first), :], vt_ref[0, :, pl.ds(o, first)], qst,
                        jnp.zeros((1, 2 * tq), F32), jnp.zeros((PAIR, 2 * tq), F32), tri_first,
                        _causal_mask_t(first, tq, o, i * tq, True))

    n_older = jnp.maximum(i - 1, 0) * (tq // tk)

    def cond(c):
        n, run, _ = c
        return (n < n_older) & (jnp.max(run) > SB_LOG_FLOOR)

    def body(c):
        n, run, acc = c
        ob = pl.multiple_of((n_older - 1 - n) * tk, tk)
        run, acc = _sb_block_t(k_ref[pl.ds(ob, tk), :], vt_ref[0, :, pl.ds(ob, tk)], qst, run, acc, tri, None)
        return n + 1, run, acc

    _, _, acc = lax.while_loop(cond, body, (jnp.int32(0),) + carry)
    o_ref[...] = _unstack_pair_t(acc, tq).astype(BF16)


def _sb_attn(qtb, kb, vtb, b, t):
    tq = tk = min(t, SB_BLOCK)
    nq = t // tq
    qt_spec, k_spec, vt_spec, o_spec = _attn_specs(t, tq, nq)
    return pl.pallas_call(
        functools.partial(_sb_kernel, tq=tq, tk=tk),
        grid=(b, N_PAIRS, nq),
        in_specs=[qt_spec, k_spec, vt_spec],
        out_specs=o_spec,
        out_shape=jax.ShapeDtypeStruct((b * t, D_HALF), BF16),
        compiler_params=_params("parallel", "parallel", "arbitrary"),
        name="sb_attn",
    )(qtb, kb, vtb)


def _page_specs(block, n):
    nd = len(block) - 1
    return [pl.BlockSpec(block, functools.partial(lambda b, g, pt, p: (pt[b, g * n + p],) + (0,) * nd, p=p))
            for p in range(n)]


def _cumf_dec_kernel(pt_ref, *refs, n):
    pages, lfn_ref, fp_ref, fn_ref, carry_ref = refs[:n], refs[n], refs[n + 1], refs[n + 2], refs[n + 3]
    g = pl.program_id(1)
    upper = _tri(PAGE, "row_le_col")

    @pl.when(g == 0)
    def _():
        carry_ref[...] = jnp.zeros_like(carry_ref)

    stacked = jnp.concatenate([pages[p][0] for p in range(n)] + [lfn_ref[0]], axis=0)
    within = _dot_f32(stacked, upper)
    base = carry_ref[:, 0:1]
    seen = jnp.zeros((N_HEADS, PAGE), F32)
    for p in range(n + 1):
        wp = within[p * N_HEADS:(p + 1) * N_HEADS]
        f = wp + (seen[:, PAGE - 1:PAGE] + base)
        if p < n:
            fp_ref[0, :, p * PAGE:(p + 1) * PAGE] = f
            seen = seen + wp
        else:
            fn_ref[0] = f
    carry_ref[...] = jnp.broadcast_to(seen[:, PAGE - 1:PAGE] + base, carry_ref.shape)


def _cumf_dec(page_table, lf_pool_t, lf_new):
    nb, n_pages = page_table.shape
    n = _row_tile(n_pages, LOGF_PAGES_PER_STEP)
    groups = n_pages // n
    return pl.pallas_call(
        functools.partial(_cumf_dec_kernel, n=n),
        grid_spec=pltpu.PrefetchScalarGridSpec(
            num_scalar_prefetch=1,
            grid=(nb, groups),
            in_specs=_page_specs((1, N_HEADS, PAGE), n) + [pl.BlockSpec((1, N_HEADS, PAGE), lambda b, g, pt: (b, 0, 0))],
            out_specs=[pl.BlockSpec((1, N_HEADS, n * PAGE), lambda b, g, pt: (b, 0, g)),
                       pl.BlockSpec((1, N_HEADS, PAGE), lambda b, g, pt: (b, 0, 0))],
            scratch_shapes=[pltpu.VMEM((N_HEADS, LANES), F32)]),
        out_shape=[jax.ShapeDtypeStruct((nb, N_HEADS, n_pages * PAGE), F32),
                   jax.ShapeDtypeStruct((nb, N_HEADS, PAGE), F32)],
        compiler_params=_params("parallel", "arbitrary"),
        name="cum_logf_paged",
    )(page_table, *([lf_pool_t] * n), lf_new)


def _block_diag_q(q):
    t = q.shape[0]
    rep = jnp.concatenate([q.astype(F32)] * N_HEADS, axis=0)
    rh = lax.broadcasted_iota(jnp.int32, (N_HEADS * t, D_HALF), 0) // t
    lh = lax.broadcasted_iota(jnp.int32, (N_HEADS * t, D_HALF), 1) // HEAD_DIM
    return jnp.where(rh == lh, rep, 0.0).astype(BF16)


def _collapse_heads(acc, t):
    rh = lax.broadcasted_iota(jnp.int32, (N_HEADS * t, D_HALF), 0) // t
    lh = lax.broadcasted_iota(jnp.int32, (N_HEADS * t, D_HALF), 1) // HEAD_DIM
    masked = jnp.where(rh == lh, acc, 0.0)
    out = masked[0:t]
    for h in range(1, N_HEADS):
        out = out + masked[h * t:(h + 1) * t]
    return out


def _rep_heads(x, t):
    return jnp.concatenate([jnp.broadcast_to(x[h:h + 1], (t, x.shape[1])) for h in range(N_HEADS)], axis=0)


def _fox_dec_kernel(pt_ref, *refs, nt):
    n = PAGES_PER_STEP
    kp, vp = refs[:n], refs[n:2 * n]
    q_ref, kn_ref, vn_ref, fp_ref, fn_ref, o_ref, ks_ref, vs_ref, m_ref, l_ref, acc_ref = refs[2 * n:]
    g = pl.program_id(1)
    rows = N_HEADS * nt

    @pl.when(g == 0)
    def _():
        m_ref[...] = jnp.full_like(m_ref, NEG_BIG)
        l_ref[...] = jnp.zeros_like(l_ref)
        acc_ref[...] = jnp.zeros_like(acc_ref)

    for p in range(n):
        ks_ref[:, p * PAGE:(p + 1) * PAGE] = kp[p][0].astype(BF16)
        vs_ref[:, p * PAGE:(p + 1) * PAGE] = vp[p][0].astype(BF16)

    qbd = _block_diag_q(q_ref[0])
    fnew = _rep_heads(fn_ref[0], nt)
    tpos = lax.broadcasted_iota(jnp.int32, (rows, LANES), 0) % nt
    lane = lax.broadcasted_iota(jnp.int32, (rows, LANES), 1)
    fq = jnp.sum(jnp.where(lane == tpos, fnew, 0.0), axis=1, keepdims=True)

    def update(s, vt):
        m_old = m_ref[:, 0:1]
        m_new = jnp.maximum(m_old, jnp.max(s, axis=1, keepdims=True))
        alpha = jnp.exp(m_old - m_new)
        p_ = jnp.exp(s - m_new)
        l_ref[...] = jnp.broadcast_to(alpha * l_ref[:, 0:1] + jnp.sum(p_, axis=1, keepdims=True), l_ref.shape)
        acc_ref[...] = alpha * acc_ref[...] + _dot_nt(p_.astype(BF16), vt)
        m_ref[...] = jnp.broadcast_to(m_new, m_ref.shape)

    s = _dot(qbd, ks_ref[...]) + (fq - _rep_heads(fp_ref[0], nt))
    update(s, vs_ref[...])

    @pl.when(g == pl.num_programs(1) - 1)
    def _():
        sn = _dot(qbd, kn_ref[0]) + (fq - fnew)
        sn = jnp.where(lane <= tpos, sn, NEG_BIG)
        update(sn, vn_ref[0])
        o_ref[0] = _collapse_heads(acc_ref[...] / l_ref[:, 0:1], nt).astype(BF16)


def _fox_dec(page_table, k_pool, v_pool, qb, knt, vnt, f_past, f_new):
    nb, n_pages = page_table.shape
    nt = qb.shape[1]
    n = PAGES_PER_STEP
    groups = n_pages // n
    rows = N_HEADS * nt
    seq = lambda b, g, pt: (b, 0, 0)
    return pl.pallas_call(
        functools.partial(_fox_dec_kernel, nt=nt),
        grid_spec=pltpu.PrefetchScalarGridSpec(
            num_scalar_prefetch=1,
            grid=(nb, groups),
            in_specs=_page_specs((1, D_HALF, PAGE), n) + _page_specs((1, D_HALF, PAGE), n)
            + [pl.BlockSpec((1, nt, D_HALF), seq), pl.BlockSpec((1, D_HALF, PAGE), seq),
               pl.BlockSpec((1, D_HALF, PAGE), seq),
               pl.BlockSpec((1, N_HEADS, n * PAGE), lambda b, g, pt: (b, 0, g)),
               pl.BlockSpec((1, N_HEADS, PAGE), seq)],
            out_specs=pl.BlockSpec((1, nt, D_HALF), seq),
            scratch_shapes=[pltpu.VMEM((D_HALF, n * PAGE), BF16), pltpu.VMEM((D_HALF, n * PAGE), BF16),
                            pltpu.VMEM((rows, LANES), F32), pltpu.VMEM((rows, LANES), F32),
                            pltpu.VMEM((rows, D_HALF), F32)]),
        out_shape=jax.ShapeDtypeStruct((nb, nt, D_HALF), BF16),
        compiler_params=_params("parallel", "arbitrary"),
        name="fox_decode",
    )(page_table, *([k_pool] * n), *([v_pool] * n), qb, knt, vnt, f_past, f_new)


def _sb_block(qs, kt, vt, run, acc, strict_tri, mask):
    z = _dot(qs, kt)
    lk = -(jnp.maximum(z, 0.0) + jnp.log(1.0 + jnp.exp(-jnp.abs(z))))
    if mask is not None:
        lk = jnp.where(mask, lk, 0.0)
    hi = lk.astype(BF16)
    lo = (lk - hi.astype(F32)).astype(BF16)
    later = _dot(hi, strict_tri) + _dot(lo, strict_tri) + run
    a = jnp.exp(z + lk + later)
    if mask is not None:
        a = jnp.where(mask, a, 0.0)
    acc = acc + _dot_nt(a.astype(BF16), vt)
    run = run + jnp.sum(lk, axis=1, keepdims=True)
    return run, acc


def _sb_dec_kernel(pt_ref, q_ref, kn_ref, vn_ref, kpool_ref, vpool_ref, o_ref, kbuf, vbuf, sem, *, nt, n_pages, gp):
    b = pl.program_id(0)
    slot = b % 2
    rows = N_HEADS * nt
    n_groups = n_pages // gp
    tri = _tri(PAGE, "row_gt_col").astype(BF16)
    qbd = _block_diag_q(q_ref[0])

    def group_copies(seq, grp, slot_):
        out = []
        for p in range(gp):
            page = pt_ref[seq, n_pages - (grp + 1) * gp + p]
            out.append(pltpu.make_async_copy(kpool_ref.at[page], kbuf.at[slot_, p], sem.at[slot_, 0]))
            out.append(pltpu.make_async_copy(vpool_ref.at[page], vbuf.at[slot_, p], sem.at[slot_, 1]))
        return out

    def visit_group(slot_, run, acc):
        for p in reversed(range(gp)):
            run, acc = _sb_block(qbd, kbuf[slot_, p].astype(BF16), vbuf[slot_, p].astype(BF16), run, acc, tri, None)
        return run, acc

    @pl.when(b == 0)
    def _():
        for c in group_copies(0, 0, 0):
            c.start()

    @pl.when(b + 1 < pl.num_programs(0))
    def _():
        for c in group_copies(b + 1, 0, 1 - slot):
            c.start()

    tpos = lax.broadcasted_iota(jnp.int32, (rows, LANES), 0) % nt
    lane = lax.broadcasted_iota(jnp.int32, (rows, LANES), 1)
    run, acc = _sb_block(qbd, kn_ref[0], vn_ref[0], jnp.zeros((rows, 1), F32), jnp.zeros((rows, D_HALF), F32),
                         tri, lane < tpos)

    for c in group_copies(b, 0, slot):
        c.wait()
    run, acc = visit_group(slot, run, acc)

    def cond(c):
        g, run, _ = c
        return (g < n_groups) & (jnp.max(run) > SB_LOG_FLOOR)

    def body(c):
        g, run, acc = c
        copies = group_copies(b, g, slot)
        for cp in copies:
            cp.start()
        for cp in copies:
            cp.wait()
        run, acc = visit_group(slot, run, acc)
        return g + 1, run, acc

    _, _, acc = lax.while_loop(cond, body, (jnp.int32(1), run, acc))
    o_ref[0] = _collapse_heads(acc, nt).astype(BF16)


def _sb_dec(page_table, k_pool, v_pool, qb, knt, vnt):
    nb, n_pages = page_table.shape
    nt = qb.shape[1]
    gp = _row_tile(n_pages, SB_PAGES_PER_FETCH)
    seq = lambda b, pt: (b, 0, 0)
    return pl.pallas_call(
        functools.partial(_sb_dec_kernel, nt=nt, n_pages=n_pages, gp=gp),
        grid_spec=pltpu.PrefetchScalarGridSpec(
            num_scalar_prefetch=1,
            grid=(nb,),
            in_specs=[pl.BlockSpec((1, nt, D_HALF), seq), pl.BlockSpec((1, D_HALF, PAGE), seq),
                      pl.BlockSpec((1, D_HALF, PAGE), seq),
                      pl.BlockSpec(memory_space=pl.ANY), pl.BlockSpec(memory_space=pl.ANY)],
            out_specs=pl.BlockSpec((1, nt, D_HALF), seq),
            scratch_shapes=[pltpu.VMEM((2, gp, D_HALF, PAGE), F32), pltpu.VMEM((2, gp, D_HALF, PAGE), F32),
                            pltpu.SemaphoreType.DMA((2, 2))]),
        out_shape=jax.ShapeDtypeStruct((nb, nt, D_HALF), BF16),
        compiler_params=_params("arbitrary"),
        name="sb_decode",
    )(page_table, qb, knt, vnt, k_pool, v_pool)


def _new_feat_major(kb, b, t):
    return jnp.pad(kb.reshape(b, t, D_HALF).transpose(0, 2, 1), ((0, 0), (0, 0), (0, PAGE - t)))


def _heads_last(kt, b, t):
    return kt.reshape(b, N_HEADS, HEAD_DIM, t).transpose(0, 3, 1, 2)[None]


def _trunk(x, p, cache):
    b, t, d = x.shape
    m = b * t
    x2 = x.reshape(m, d)
    prompt = cache is None

    proj = _proj_even(x2, p["g_mix"][0], p["w_in_even"][0], p["b_forget"][0], p["sgu_g"][0], p["sgu_b"][0],
                      b, t, prompt)
    lfc, lfr, u, vn, vnb = proj[5:]
    w_s, b_s = p["w_spatial"][0].astype(F32), p["b_spatial"][0].astype(F32)
    if prompt:
        qt, kb, kt, vt, vtb = proj[:5]
        frow, kcat = _cumf(lfc, lfr, kb, b, t)
        a = _fox_attn(qt, kcat, vtb, frow, b, t)
        mask = np.tril(np.ones((CHUNK, CHUNK), np.float32))
        bias = jnp.repeat(b_s.T, HEAD_DIM, axis=1)
        w_mix = w_s
        even_rows = (_heads_last(kt, b, t), _heads_last(vt, b, t), lfr.transpose(0, 2, 1)[None], None)
    else:
        qb, k, kb, v, vb = proj[:5]
        pt = cache["page_table"]
        lf_new = jnp.pad(lfr.reshape(N_HEADS, b, t).transpose(1, 0, 2), ((0, 0), (0, 0), (0, PAGE - t)))
        f_past, f_new = _cumf_dec(pt, cache["fox_logf_t"], lf_new)
        a = _fox_dec(pt, cache["fox_k"], cache["fox_v"], qb.reshape(b, t, D_HALF),
                     _new_feat_major(kb, b, t), _new_feat_major(vb, b, t), f_past, f_new).reshape(m, D_HALF)
        reps = CHUNK // t
        idx = np.arange(CHUNK)
        mask = ((idx[:, None] // t == idx[None, :] // t) & (idx[None, :] % t <= idx[:, None] % t)).astype(np.float32)
        w_mix = jnp.tile(w_s[:, :t, :t], (1, reps, reps))
        bias = jnp.tile(jnp.repeat(b_s.T[:t], HEAD_DIM, axis=1), (reps, 1))
        hd = lambda z: z.reshape(1, b, t, N_HEADS, HEAD_DIM)
        even_rows = (hd(k), hd(v), lfc[:, :N_HEADS].reshape(1, b, t, N_HEADS), vn.reshape(1, b, t, D_HALF))
    ob = _sgu(vnb, u, w_mix, jnp.asarray(mask), bias)
    x2 = _mix_mlp(x2, a, ob, p["w_out_even"][0], p["g_mlp"][0], p["w_up"][0], p["w_down"][0], p["g_final"], False)

    proj = _proj_odd(x2, p["g_mix"][1], p["w_in_odd"][0], b, t, prompt)
    gb, uc = proj[5:]
    if prompt:
        qt, kb, kt, vt, vtb = proj[:5]
        a = _sb_attn(qt, kb, vtb, b, t)
        prev = jnp.zeros((b, CONV_W - 1, D_HALF), F32)
        kv_rows = (_heads_last(kt, b, t), _heads_last(vt, b, t))
    else:
        qb, k, kb, v, vb = proj[:5]
        a = _sb_dec(cache["page_table"], cache["sb_k"], cache["sb_v"], qb.reshape(b, t, D_HALF),
                    _new_feat_major(kb, b, t), _new_feat_major(vb, b, t)).reshape(m, D_HALF)
        prev = cache["conv"]
        kv_rows = (k.reshape(1, b, t, N_HEADS, HEAD_DIM), v.reshape(1, b, t, N_HEADS, HEAD_DIM))
    od, new_conv = _conv(uc.reshape(b, t, D_HALF), gb.reshape(b, t, D_HALF), prev, p["conv_w"][0])
    y = _mix_mlp(x2, a, od.reshape(m, D_HALF), p["w_out_odd"][0], p["g_mlp"][1], p["w_up"][1], p["w_down"][1],
                 p["g_final"], True)
    return y.reshape(b, t, d), even_rows, kv_rows + (new_conv[None],)


def _pool_feat_major(cache):
    n_pool = cache.shape[1]
    return cache[0].transpose(0, 2, 3, 1).reshape(n_pool, D_HALF, PAGE)


def kernel(x_prompt, x_sample, cache_fox_k, cache_fox_v, cache_fox_logf, cache_sb_k, cache_sb_v, state_conv,
           page_table, g_mix, g_mlp, g_final, w_up, w_down, w_in_even, b_forget, sgu_g, sgu_b, w_spatial,
           b_spatial, w_out_even, w_in_odd, conv_w, w_out_odd):
    p = dict(g_mix=g_mix, g_mlp=g_mlp, g_final=g_final, w_up=w_up, w_down=w_down, w_in_even=w_in_even,
             b_forget=b_forget, sgu_g=sgu_g, sgu_b=sgu_b, w_spatial=w_spatial, b_spatial=b_spatial,
             w_out_even=w_out_even, w_in_odd=w_in_odd, conv_w=conv_w, w_out_odd=w_out_odd)
    y_p, (pk, pv, plf, _), (psk, psv, pconv) = _trunk(x_prompt, p, None)
    cache = dict(page_table=page_table,
                 fox_k=_pool_feat_major(cache_fox_k), fox_v=_pool_feat_major(cache_fox_v),
                 fox_logf_t=cache_fox_logf[0].transpose(0, 2, 1),
                 sb_k=_pool_feat_major(cache_sb_k), sb_v=_pool_feat_major(cache_sb_v),
                 conv=state_conv[0])
    y_s, (sk, sv, slf, svn), (ssk, ssv, sconv) = _trunk(x_sample, p, cache)
    return (y_p, y_s, pk, pv, plf, psk, psv, pconv, sk, sv, slf, ssk, ssv, sconv, svn)
```

```python
import functools

import numpy as np
import jax
import jax.numpy as jnp
from jax import lax
from jax.experimental import pallas as pl
from jax.experimental.pallas import tpu as pltpu

F32 = jnp.float32
BF16 = jnp.bfloat16

HEAD_DIM = 64
N_HEADS = 8
PAIR = 2 * HEAD_DIM
N_PAIRS = N_HEADS // 2
D_HALF = N_HEADS * HEAD_DIM
CHUNK = 128
PAGE = 128
CONV_W = 3
EPS = 1e-6
QK_SCALE = HEAD_DIM ** -0.5
NEG_BIG = -1e30
LANES = 128
VMEM_LIMIT = 56 * 1024 * 1024
PAGES_PER_STEP = 32
LOGF_PAGES_PER_STEP = 64
SB_PAGES_PER_FETCH = 2
FOX_BLOCK = 1024
FOX_CHAINS = 2
SB_BLOCK = 256
SB_LOG_FLOOR = -110.0

NT_DIMS = (((1,), (1,)), ((), ()))


def _dot(a, b):
    return jnp.dot(a, b, preferred_element_type=F32)


def _dot_nt(a, b):
    return lax.dot_general(a, b, NT_DIMS, preferred_element_type=F32)


def _dot_f32(a, b):
    return jnp.dot(a, b, preferred_element_type=F32, precision=lax.Precision.HIGHEST)


def _rms(x, g):
    return x * lax.rsqrt(jnp.mean(x * x, axis=-1, keepdims=True) + EPS) * g


def _log_sigmoid(x):
    return jnp.minimum(x, 0.0) - jnp.log1p(jnp.exp(-jnp.abs(x)))


def _params(*sem):
    return pltpu.CompilerParams(dimension_semantics=sem, vmem_limit_bytes=VMEM_LIMIT)


def _resident(shape):
    nd = len(shape)
    return pl.BlockSpec(shape, lambda *_: (0,) * nd, pipeline_mode=pl.Buffered(1))


def _row_tile(m, cap):
    t = min(m, cap)
    assert m % t == 0
    return t


def _tri(n, kind):
    r = lax.broadcasted_iota(jnp.int32, (n, n), 0)
    c = lax.broadcasted_iota(jnp.int32, (n, n), 1)
    keep = {"row_le_col": r <= c, "row_ge_col": r >= c, "row_gt_col": r > c, "row_lt_col": r < c}[kind]
    return jnp.where(keep, 1.0, 0.0).astype(F32)


def _emit_qkv(hb, w_refs, o_refs, prompt):
    if prompt:
        wqt, wk, wkt, wvt = w_refs
        qt_ref, kb_ref, kt_ref, vt_ref, vtb_ref = o_refs
        qt_ref[0] = (_dot_nt(wqt[...], hb) * QK_SCALE).astype(BF16)
        kb_ref[...] = _dot(hb, wk[...]).astype(BF16)
        kt_ref[0] = _dot_nt(wkt[...], hb)
        vt = _dot_nt(wvt[...], hb)
        vt_ref[0] = vt
        vtb_ref[0] = vt.astype(BF16)
    else:
        wq, wk, wv = w_refs
        q_ref, k_ref, kb_ref, v_ref, vb_ref = o_refs
        q_ref[...] = (_dot(hb, wq[...]) * QK_SCALE).astype(BF16)
        k = _dot(hb, wk[...])
        k_ref[...] = k
        kb_ref[...] = k.astype(BF16)
        v = _dot(hb, wv[...])
        v_ref[...] = v
        vb_ref[...] = v.astype(BF16)


def _qkv_weights(wq, wk, wv, prompt):
    return [wq.T, wk, wk.T, wv.T] if prompt else [wq, wk, wv]


def _tok_out(m, tm, w, dt):
    return jax.ShapeDtypeStruct((m, w), dt), pl.BlockSpec((tm, w), lambda i: (i, 0))


def _feat_out(b, t, tm, w, dt):
    nt = t // tm
    return jax.ShapeDtypeStruct((b, w, t), dt), pl.BlockSpec((1, w, tm), lambda i: (i // nt, 0, i % nt))


def _qkv_outs(b, t, tm, prompt):
    m = b * t
    if prompt:
        return [_feat_out(b, t, tm, D_HALF, BF16), _tok_out(m, tm, D_HALF, BF16), _feat_out(b, t, tm, D_HALF, F32),
                _feat_out(b, t, tm, D_HALF, F32), _feat_out(b, t, tm, D_HALF, BF16)]
    return [_tok_out(m, tm, D_HALF, BF16), _tok_out(m, tm, D_HALF, F32), _tok_out(m, tm, D_HALF, BF16),
            _tok_out(m, tm, D_HALF, F32), _tok_out(m, tm, D_HALF, BF16)]


def _proj_even_kernel(*refs, prompt):
    nw = 4 if prompt else 3
    x_ref, g_ref = refs[:2]
    w_refs = refs[2:2 + nw]
    wf_ref, wft_ref, wu_ref, wg_ref, bfc_ref, bfr_ref, sg_ref, sb_ref = refs[2 + nw:10 + nw]
    o_refs = refs[10 + nw:15 + nw]
    lfc_ref, lfr_ref, u_ref, vn_ref, vnb_ref = refs[15 + nw:]
    hb = _rms(x_ref[...], g_ref[...]).astype(BF16)
    _emit_qkv(hb, w_refs, o_refs, prompt)
    lfc_ref[...] = _log_sigmoid(_dot(hb, wf_ref[...]) + bfc_ref[...])
    lfr = _log_sigmoid(_dot_nt(wft_ref[...], hb)[:N_HEADS] + bfr_ref[...])
    if prompt:
        lfr_ref[0] = lfr
    else:
        lfr_ref[...] = lfr
    u_ref[...] = _dot(hb, wu_ref[...])
    vg = _dot(hb, wg_ref[...])
    mu = jnp.mean(vg, axis=-1, keepdims=True)
    vc = vg - mu
    var = jnp.mean(vc * vc, axis=-1, keepdims=True)
    vn = vc * lax.rsqrt(var + EPS) * sg_ref[...] + sb_ref[...]
    vn_ref[...] = vn
    vnb_ref[...] = vn.astype(BF16)


def _proj_even(x, g, w_in, b_forget, sgu_g, sgu_b, b, t, prompt):
    m, d = x.shape
    tm = _row_tile(t if prompt else m, 512)
    wb = w_in.astype(BF16)
    o = 3 * D_HALF
    qkv_w = _qkv_weights(wb[:, :D_HALF], wb[:, D_HALF:2 * D_HALF], wb[:, 2 * D_HALF:o], prompt)
    wf = jnp.pad(wb[:, o:o + N_HEADS], ((0, 0), (0, LANES - N_HEADS)))
    wft = jnp.pad(wb[:, o:o + N_HEADS].T, ((0, 16 - N_HEADS), (0, 0)))
    wu, wg = wb[:, o + N_HEADS:o + N_HEADS + D_HALF], wb[:, o + N_HEADS + D_HALF:]
    bfc = jnp.pad(b_forget.astype(F32), (0, LANES - N_HEADS)).reshape(1, LANES)
    bfr = b_forget.astype(F32).reshape(N_HEADS, 1)
    lfr_out = (_feat_out(b, t, tm, N_HEADS, F32) if prompt else
               (jax.ShapeDtypeStruct((N_HEADS, m), F32), pl.BlockSpec((N_HEADS, tm), lambda i: (0, i))))
    outs = (_qkv_outs(b, t, tm, prompt)
            + [_tok_out(m, tm, LANES, F32), lfr_out,
               _tok_out(m, tm, D_HALF, F32), _tok_out(m, tm, D_HALF, F32), _tok_out(m, tm, D_HALF, BF16)])
    weights = qkv_w + [wf, wft, wu, wg, bfc, bfr,
                       sgu_g.astype(F32).reshape(1, D_HALF), sgu_b.astype(F32).reshape(1, D_HALF)]
    return pl.pallas_call(
        functools.partial(_proj_even_kernel, prompt=prompt),
        grid=(m // tm,),
        in_specs=[pl.BlockSpec((tm, d), lambda i: (i, 0)), _resident((1, d))] + [_resident(w.shape) for w in weights],
        out_specs=[s for _, s in outs],
        out_shape=[s for s, _ in outs],
        compiler_params=_params("parallel"),
        name="proj_even",
    )(x, g.astype(F32).reshape(1, d), *weights)


def _proj_odd_kernel(*refs, prompt):
    nw = 4 if prompt else 3
    x_ref, g_ref = refs[:2]
    w_refs = refs[2:2 + nw]
    wgb_ref, wgc_ref, wh_ref = refs[2 + nw:5 + nw]
    o_refs = refs[5 + nw:10 + nw]
    gb_ref, u_ref = refs[10 + nw:]
    hb = _rms(x_ref[...], g_ref[...]).astype(BF16)
    _emit_qkv(hb, w_refs, o_refs, prompt)
    gb_ref[...] = _dot(hb, wgb_ref[...])
    u_ref[...] = _dot(hb, wgc_ref[...]) * _dot(hb, wh_ref[...])


def _proj_odd(x, g, w_in, b, t, prompt):
    m, d = x.shape
    tm = _row_tile(t if prompt else m, 512)
    wb = w_in.astype(BF16)
    ws = [wb[:, i * D_HALF:(i + 1) * D_HALF] for i in range(6)]
    weights = _qkv_weights(ws[0], ws[1], ws[2], prompt) + ws[3:]
    outs = _qkv_outs(b, t, tm, prompt) + [_tok_out(m, tm, D_HALF, F32), _tok_out(m, tm, D_HALF, F32)]
    return pl.pallas_call(
        functools.partial(_proj_odd_kernel, prompt=prompt),
        grid=(m // tm,),
        in_specs=[pl.BlockSpec((tm, d), lambda i: (i, 0)), _resident((1, d))] + [_resident(w.shape) for w in weights],
        out_specs=[s for _, s in outs],
        out_shape=[s for s, _ in outs],
        compiler_params=_params("parallel"),
        name="proj_odd",
    )(x, g.astype(F32).reshape(1, d), *weights)


def _mix_mlp_kernel(x_ref, a_ref, b_ref, woa_ref, wob_ref, g_ref, wup_ref, wdn_ref, gf_ref, o_ref,
                    *, final_norm, ff_chunk):
    x1 = x_ref[...] + _dot(a_ref[...], woa_ref[...]) + _dot(b_ref[...], wob_ref[...])
    hb = _rms(x1, g_ref[...]).astype(BF16)
    y = None
    for c in range(wup_ref.shape[1] // ff_chunk):
        sl = slice(c * ff_chunk, (c + 1) * ff_chunk)
        up = jnp.maximum(_dot(hb, wup_ref[:, sl]), 0.0)
        down = _dot((up * up).astype(BF16), wdn_ref[sl, :])
        y = down if y is None else y + down
    out = x1 + y
    o_ref[...] = _rms(out, gf_ref[...]) if final_norm else out


def _mix_mlp(x, a, b, w_out, g_mlp, w_up, w_down, g_final, final_norm):
    m, d = x.shape
    tm = _row_tile(m, 512)
    wo = w_out.astype(BF16)
    weights = [wo[:D_HALF], wo[D_HALF:], g_mlp.astype(F32).reshape(1, d), w_up.astype(BF16),
               w_down.astype(BF16), g_final.astype(F32).reshape(1, d)]
    row = lambda i: (i, 0)
    return pl.pallas_call(
        functools.partial(_mix_mlp_kernel, final_norm=final_norm, ff_chunk=1024),
        grid=(m // tm,),
        in_specs=[pl.BlockSpec((tm, d), row), pl.BlockSpec((tm, D_HALF), row), pl.BlockSpec((tm, D_HALF), row)]
        + [_resident(w.shape) for w in weights],
        out_specs=pl.BlockSpec((tm, d), row),
        out_shape=jax.ShapeDtypeStruct((m, d), F32),
        compiler_params=_params("parallel"),
        name="mix_mlp",
    )(x, a, b, *weights)


def _sgu_kernel(vn_ref, u_ref, w_ref, mask_ref, bias_ref, o_ref):
    lo = lax.broadcasted_iota(jnp.int32, (1, PAIR), 1) < HEAD_DIM
    keep = mask_ref[...] > 0.0
    for gp in range(N_PAIRS):
        we = jnp.where(keep, w_ref[2 * gp], 0.0).astype(BF16)
        wo = jnp.where(keep, w_ref[2 * gp + 1], 0.0).astype(BF16)
        cols = slice(gp * PAIR, (gp + 1) * PAIR)
        for c in range(vn_ref.shape[0] // CHUNK):
            rows = slice(c * CHUNK, (c + 1) * CHUNK)
            vn2 = vn_ref[rows, cols]
            mixed = jnp.where(lo, _dot(we, vn2), _dot(wo, vn2)) + bias_ref[:, cols]
            o_ref[rows, cols] = (u_ref[rows, cols] * mixed).astype(BF16)


def _sgu(vnb, u, w, mask, bias):
    m = vnb.shape[0]
    assert m % CHUNK == 0
    tm = _row_tile(m, 512)
    row = lambda i: (i, 0)
    return pl.pallas_call(
        _sgu_kernel,
        grid=(m // tm,),
        in_specs=[pl.BlockSpec((tm, D_HALF), row), pl.BlockSpec((tm, D_HALF), row),
                  _resident(w.shape), _resident(mask.shape), _resident(bias.shape)],
        out_specs=pl.BlockSpec((tm, D_HALF), row),
        out_shape=jax.ShapeDtypeStruct((m, D_HALF), BF16),
        compiler_params=_params("parallel"),
        name="sgu",
    )(vnb, u, w, mask, bias)


def _conv_kernel(u_ref, gb_ref, prev_ref, w_ref, o_ref, new_ref, carry_ref):
    i = pl.program_id(1)

    @pl.when(i == 0)
    def _():
        carry_ref[...] = jnp.zeros_like(carry_ref)
        carry_ref[6:8, :] = prev_ref[0]

    u = u_ref[0]
    tt = u.shape[0]
    prev = carry_ref[...]
    r = lax.broadcasted_iota(jnp.int32, (tt, 1), 0)
    u1 = pltpu.roll(u, 1, 0)
    u2 = pltpu.roll(u, 2, 0)
    p1 = jnp.broadcast_to(prev[7:8], u.shape)
    p2 = jnp.where(r == 0, jnp.broadcast_to(prev[6:7], u.shape), p1)
    u1 = jnp.where(r == 0, p1, u1)
    u2 = jnp.where(r <= 1, p2, u2)
    conv = w_ref[0:1] * u2 + w_ref[1:2] * u1 + w_ref[2:3] * u
    o_ref[0] = (gb_ref[0] * conv).astype(BF16)
    if tt >= 8:
        tail = u[tt - 8:tt]
    else:
        tail = jnp.concatenate([prev[tt:8], u], axis=0)
    carry_ref[...] = tail
    new_ref[0] = tail[6:8]


def _conv(u, gb, prev, conv_w):
    b, t, d = u.shape
    tt = _row_tile(t, 1024)
    blk = lambda bi, i: (bi, i, 0)
    return pl.pallas_call(
        _conv_kernel,
        grid=(b, t // tt),
        in_specs=[pl.BlockSpec((1, tt, d), blk), pl.BlockSpec((1, tt, d), blk),
                  pl.BlockSpec((1, CONV_W - 1, d), lambda bi, i: (bi, 0, 0)), _resident((CONV_W, d))],
        out_specs=[pl.BlockSpec((1, tt, d), blk), pl.BlockSpec((1, CONV_W - 1, d), lambda bi, i: (bi, 0, 0))],
        out_shape=[jax.ShapeDtypeStruct((b, t, d), BF16), jax.ShapeDtypeStruct((b, CONV_W - 1, d), F32)],
        scratch_shapes=[pltpu.VMEM((8, d), F32)],
        compiler_params=_params("parallel", "arbitrary"),
        name="short_conv",
    )(u, gb, prev.astype(F32), conv_w.astype(F32))


AUG_ONE0 = 6


def _split3(x):
    hi = x.astype(BF16).astype(F32)
    r = x - hi
    mid = r.astype(BF16).astype(F32)
    lo = (r - mid).astype(BF16).astype(F32)
    return hi, mid, lo


def _cumf_kernel(lfc_ref, lfr_ref, k_ref, fr_ref, kcat_ref):
    t = lfc_ref.shape[0]
    lower = _tri(CHUNK, "row_ge_col")
    upper = _tri(CHUNK, "row_le_col")
    src = lax.broadcasted_iota(jnp.int32, (LANES, LANES), 0)
    dst = lax.broadcasted_iota(jnp.int32, (LANES, LANES), 1)
    lane = lax.broadcasted_iota(jnp.int32, (1, LANES), 1)
    ones = jnp.where((lane >= AUG_ONE0) & (lane < AUG_ONE0 + 3), 1.0, 0.0)
    sel = [[jnp.where(((src == 2 * hp) & (dst == c)) | ((src == 2 * hp + 1) & (dst == 3 + c)), 1.0, 0.0).astype(BF16)
            for c in range(3)] for hp in range(N_PAIRS)]

    def body(c, carry):
        cc, cr = carry
        o = pl.multiple_of(c * CHUNK, CHUNK)
        fc = _dot_f32(lower, lfc_ref[pl.ds(o, CHUNK), :]) + cc
        fr = _dot_f32(lfr_ref[0, :, pl.ds(o, CHUNK)], upper) + cr
        fr_ref[0, :, pl.ds(o, CHUNK)] = fr
        parts = [p.astype(BF16) for p in _split3(fc)]
        for hp in range(N_PAIRS):
            aug = _dot(parts[0], sel[hp][0]) + _dot(parts[1], sel[hp][1]) + _dot(parts[2], sel[hp][2]) + ones
            kcat_ref[pl.ds(o, CHUNK), 2 * hp * LANES:(2 * hp + 1) * LANES] = k_ref[pl.ds(o, CHUNK), hp * LANES:(hp + 1) * LANES]
            kcat_ref[pl.ds(o, CHUNK), (2 * hp + 1) * LANES:(2 * hp + 2) * LANES] = aug.astype(BF16)
        return fc[CHUNK - 1:CHUNK, :], fr[:, CHUNK - 1:CHUNK]

    lax.fori_loop(0, t // CHUNK, body, (jnp.zeros((1, LANES), F32), jnp.zeros((N_HEADS, 1), F32)))


def _cumf(lfc, lfr, kb, b, t):
    return pl.pallas_call(
        _cumf_kernel,
        grid=(b,),
        in_specs=[pl.BlockSpec((t, LANES), lambda i: (i, 0)), pl.BlockSpec((1, N_HEADS, t), lambda i: (i, 0, 0)),
                  pl.BlockSpec((t, D_HALF), lambda i: (i, 0))],
        out_specs=[pl.BlockSpec((1, N_HEADS, t), lambda i: (i, 0, 0)),
                   pl.BlockSpec((t, 2 * D_HALF), lambda i: (i, 0))],
        out_shape=[jax.ShapeDtypeStruct((b, N_HEADS, t), F32),
                   jax.ShapeDtypeStruct((b * t, 2 * D_HALF), BF16)],
        compiler_params=_params("parallel"),
        name="cum_logf",
    )(lfc, lfr, kb)


def _stack_pair_t(qt):
    lo = lax.broadcasted_iota(jnp.int32, (PAIR, 1), 0) < HEAD_DIM
    zero = jnp.zeros_like(qt)
    return jnp.concatenate([jnp.where(lo, qt, zero), jnp.where(lo, zero, qt)], axis=1)


def _unstack_pair_t(acc, tq):
    lo = lax.broadcasted_iota(jnp.int32, (PAIR, 1), 0) < HEAD_DIM
    return jnp.where(lo, acc[:, :tq], acc[:, tq:]).T


def _causal_mask_t(tk, tq, key0, q0, strict):
    kpos = key0 + lax.broadcasted_iota(jnp.int32, (tk, tq), 0)
    qpos = q0 + lax.broadcasted_iota(jnp.int32, (tk, tq), 1)
    ok = (kpos < qpos) if strict else (kpos <= qpos)
    return jnp.concatenate([ok, ok], axis=1)


def _fox_kernel(qt_ref, kcat_ref, vt_ref, fq_ref, o_ref, *, tq, tk, nc):
    i = pl.program_id(2)
    r = lax.broadcasted_iota(jnp.int32, (LANES, tq), 0)

    def qcat_of(c):
        qst = _stack_pair_t(qt_ref[0, c * PAIR:(c + 1) * PAIR, :])
        fq = fq_ref[0, 0, 2 * c:2 * c + 2, :]

        def qaug_half(parity):
            parts = _split3(fq[parity:parity + 1])
            x = jnp.where((r >= 3 * parity) & (r < 3 * parity + 3), -1.0, 0.0)
            for k in range(3):
                x = jnp.where(r == AUG_ONE0 + k, jnp.broadcast_to(parts[k], (LANES, tq)), x)
            return x

        qaug = jnp.concatenate([qaug_half(0), qaug_half(1)], axis=1).astype(BF16)
        return jnp.concatenate([qst, qaug], axis=0)

    qcats = [qcat_of(c) for c in range(nc)]

    def block_one(c, o, carry, masked):
        m, l, acc = carry
        s = _dot(kcat_ref[pl.ds(o, tk), c * 2 * PAIR:(c + 1) * 2 * PAIR], qcats[c])
        if masked:
            s = jnp.where(_causal_mask_t(tk, tq, o, i * tq, False), s, NEG_BIG)
        m_new = jnp.maximum(m, jnp.max(s, axis=0, keepdims=True))
        alpha = jnp.exp(m - m_new)
        p = jnp.exp(s - m_new)
        l = alpha * l + jnp.sum(p, axis=0, keepdims=True)
        acc = alpha * acc + _dot(vt_ref[0, c * PAIR:(c + 1) * PAIR, pl.ds(o, tk)], p.astype(BF16))
        return m_new, l, acc

    def block(j, carries, masked):
        o = pl.multiple_of(j * tk, tk)
        return tuple(block_one(c, o, carries[c], masked) for c in range(nc))

    init = (jnp.full((1, 2 * tq), NEG_BIG, F32), jnp.zeros((1, 2 * tq), F32), jnp.zeros((PAIR, 2 * tq), F32))
    per = tq // tk
    carries = lax.fori_loop(0, i * per, functools.partial(block, masked=False), (init,) * nc)
    for d in range(per):
        carries = block(i * per + d, carries, True)
    for c in range(nc):
        _, l, acc = carries[c]
        o_ref[:, c * PAIR:(c + 1) * PAIR] = _unstack_pair_t(acc / l, tq).astype(BF16)


def _attn_specs(t, tq, nq):
    qt_spec = pl.BlockSpec((1, PAIR, tq), lambda bi, hp, i: (bi, hp, i))
    k_spec = pl.BlockSpec((t, PAIR), lambda bi, hp, i: (bi, hp))
    vt_spec = pl.BlockSpec((1, PAIR, t), lambda bi, hp, i: (bi, hp, 0))
    o_spec = pl.BlockSpec((tq, PAIR), lambda bi, hp, i: (bi * nq + i, hp))
    return qt_spec, k_spec, vt_spec, o_spec


def _fox_attn(qtb, kcat, vtb, frow, b, t):
    tq = tk = min(t, FOX_BLOCK)
    nq = t // tq
    nc = FOX_CHAINS
    return pl.pallas_call(
        functools.partial(_fox_kernel, tq=tq, tk=tk, nc=nc),
        grid=(b, N_PAIRS // nc, nq),
        in_specs=[pl.BlockSpec((1, nc * PAIR, tq), lambda bi, g, i: (bi, g, i)),
                  pl.BlockSpec((t, nc * 2 * PAIR), lambda bi, g, i: (bi, g)),
                  pl.BlockSpec((1, nc * PAIR, t), lambda bi, g, i: (bi, g, 0)),
                  pl.BlockSpec((1, 1, 2 * nc, tq), lambda bi, g, i: (bi, g, 0, i))],
        out_specs=pl.BlockSpec((tq, nc * PAIR), lambda bi, g, i: (bi * nq + i, g)),
        out_shape=jax.ShapeDtypeStruct((b * t, D_HALF), BF16),
        compiler_params=_params("parallel", "parallel", "arbitrary"),
        name="fox_attn",
    )(qtb, kcat, vtb, frow.reshape(b, N_PAIRS // nc, 2 * nc, t))


def _sb_block_t(k, vt, qst, run, acc, tri, mask):
    z = _dot(k, qst)
    lk = -(jnp.maximum(z, 0.0) + jnp.log(1.0 + jnp.exp(-jnp.abs(z))))
    if mask is not None:
        lk = jnp.where(mask, lk, 0.0)
    hi = lk.astype(BF16)
    lo = (lk - hi.astype(F32)).astype(BF16)
    later = _dot(tri, hi) + _dot(tri, lo) + run
    a = jnp.exp(z + lk + later)
    if mask is not None:
        a = jnp.where(mask, a, 0.0)
    acc = acc + _dot(vt, a.astype(BF16))
    run = run + jnp.sum(lk, axis=0, keepdims=True)
    return run, acc


def _sb_kernel(qt_ref, k_ref, vt_ref, o_ref, *, tq, tk):
    i = pl.program_id(2)
    qst = _stack_pair_t(qt_ref[0])
    tri = _tri(tk, "row_lt_col").astype(BF16)
    first = min(2 * tq, k_ref.shape[0])
    tri_first = _tri(first, "row_lt_col").astype(BF16)
    o = pl.multiple_of(jnp.maximum(i - 1, 0) * tq, tq)
    carry = _sb_block_t(k_ref[pl.ds(o, first), :], vt_ref[0, :, pl.ds(o, first)], qst,
                        jnp.zeros((1, 2 * tq), F32), jnp.zeros((PAIR, 2 * tq), F32), tri_first,
                        _causal_mask_t(first, tq, o, i * tq, True))

    n_older = jnp.maximum(i - 1, 0) * (tq // tk)

    def cond(c):
        n, run, _ = c
        return (n < n_older) & (jnp.max(run) > SB_LOG_FLOOR)

    def body(c):
        n, run, acc = c
        ob = pl.multiple_of((n_older - 1 - n) * tk, tk)
        run, acc = _sb_block_t(k_ref[pl.ds(ob, tk), :], vt_ref[0, :, pl.ds(ob, tk)], qst, run, acc, tri, None)
        return n + 1, run, acc

    _, _, acc = lax.while_loop(cond, body, (jnp.int32(0),) + carry)
    o_ref[...] = _unstack_pair_t(acc, tq).astype(BF16)


def _sb_attn(qtb, kb, vtb, b, t):
    tq = tk = min(t, SB_BLOCK)
    nq = t // tq
    qt_spec, k_spec, vt_spec, o_spec = _attn_specs(t, tq, nq)
    return pl.pallas_call(
        functools.partial(_sb_kernel, tq=tq, tk=tk),
        grid=(b, N_PAIRS, nq),
        in_specs=[qt_spec, k_spec, vt_spec],
        out_specs=o_spec,
        out_shape=jax.ShapeDtypeStruct((b * t, D_HALF), BF16),
        compiler_params=_params("parallel", "parallel", "arbitrary"),
        name="sb_attn",
    )(qtb, kb, vtb)


def _page_specs(block, n):
    nd = len(block) - 1
    return [pl.BlockSpec(block, functools.partial(lambda b, g, pt, p: (pt[b, g * n + p],) + (0,) * nd, p=p))
            for p in range(n)]


def _cumf_dec_kernel(pt_ref, *refs, n):
    pages, lfn_ref, fp_ref, fn_ref, carry_ref = refs[:n], refs[n], refs[n + 1], refs[n + 2], refs[n + 3]
    g = pl.program_id(1)
    upper = _tri(PAGE, "row_le_col")

    @pl.when(g == 0)
    def _():
        carry_ref[...] = jnp.zeros_like(carry_ref)

    stacked = jnp.concatenate([pages[p][0] for p in range(n)] + [lfn_ref[0]], axis=0)
    within = _dot_f32(stacked, upper)
    base = carry_ref[:, 0:1]
    seen = jnp.zeros((N_HEADS, PAGE), F32)
    for p in range(n + 1):
        wp = within[p * N_HEADS:(p + 1) * N_HEADS]
        f = wp + (seen[:, PAGE - 1:PAGE] + base)
        if p < n:
            fp_ref[0, :, p * PAGE:(p + 1) * PAGE] = f
            seen = seen + wp
        else:
            fn_ref[0] = f
    carry_ref[...] = jnp.broadcast_to(seen[:, PAGE - 1:PAGE] + base, carry_ref.shape)


def _cumf_dec(page_table, lf_pool_t, lf_new):
    nb, n_pages = page_table.shape
    n = _row_tile(n_pages, LOGF_PAGES_PER_STEP)
    groups = n_pages // n
    return pl.pallas_call(
        functools.partial(_cumf_dec_kernel, n=n),
        grid_spec=pltpu.PrefetchScalarGridSpec(
            num_scalar_prefetch=1,
            grid=(nb, groups),
            in_specs=_page_specs((1, N_HEADS, PAGE), n) + [pl.BlockSpec((1, N_HEADS, PAGE), lambda b, g, pt: (b, 0, 0))],
            out_specs=[pl.BlockSpec((1, N_HEADS, n * PAGE), lambda b, g, pt: (b, 0, g)),
                       pl.BlockSpec((1, N_HEADS, PAGE), lambda b, g, pt: (b, 0, 0))],
            scratch_shapes=[pltpu.VMEM((N_HEADS, LANES), F32)]),
        out_shape=[jax.ShapeDtypeStruct((nb, N_HEADS, n_pages * PAGE), F32),
                   jax.ShapeDtypeStruct((nb, N_HEADS, PAGE), F32)],
        compiler_params=_params("parallel", "arbitrary"),
        name="cum_logf_paged",
    )(page_table, *([lf_pool_t] * n), lf_new)


def _block_diag_q(q):
    t = q.shape[0]
    rep = jnp.concatenate([q.astype(F32)] * N_HEADS, axis=0)
    rh = lax.broadcasted_iota(jnp.int32, (N_HEADS * t, D_HALF), 0) // t
    lh = lax.broadcasted_iota(jnp.int32, (N_HEADS * t, D_HALF), 1) // HEAD_DIM
    return jnp.where(rh == lh, rep, 0.0).astype(BF16)


def _collapse_heads(acc, t):
    rh = lax.broadcasted_iota(jnp.int32, (N_HEADS * t, D_HALF), 0) // t
    lh = lax.broadcasted_iota(jnp.int32, (N_HEADS * t, D_HALF), 1) // HEAD_DIM
    masked = jnp.where(rh == lh, acc, 0.0)
    out = masked[0:t]
    for h in range(1, N_HEADS):
        out = out + masked[h * t:(h + 1) * t]
    return out


def _rep_heads(x, t):
    return jnp.concatenate([jnp.broadcast_to(x[h:h + 1], (t, x.shape[1])) for h in range(N_HEADS)], axis=0)


def _fox_dec_kernel(pt_ref, *refs, nt):
    n = PAGES_PER_STEP
    kp, vp = refs[:n], refs[n:2 * n]
    q_ref, kn_ref, vn_ref, fp_ref, fn_ref, o_ref, m_ref, l_ref, acc_ref = refs[2 * n:]
    g = pl.program_id(1)
    rows = N_HEADS * nt

    @pl.when(g == 0)
    def _():
        m_ref[...] = jnp.full_like(m_ref, NEG_BIG)
        l_ref[...] = jnp.zeros_like(l_ref)
        acc_ref[...] = jnp.zeros_like(acc_ref)

    qbd = _block_diag_q(q_ref[0])
    fnew = _rep_heads(fn_ref[0], nt)
    tpos = lax.broadcasted_iota(jnp.int32, (rows, LANES), 0) % nt
    lane = lax.broadcasted_iota(jnp.int32, (rows, LANES), 1)
    fq = jnp.sum(jnp.where(lane == tpos, fnew, 0.0), axis=1, keepdims=True)

    def update(s, pv_of):
        m_old = m_ref[:, 0:1]
        m_new = jnp.maximum(m_old, jnp.max(s, axis=1, keepdims=True))
        alpha = jnp.exp(m_old - m_new)
        p_ = jnp.exp(s - m_new)
        l_ref[...] = jnp.broadcast_to(alpha * l_ref[:, 0:1] + jnp.sum(p_, axis=1, keepdims=True), l_ref.shape)
        acc_ref[...] = alpha * acc_ref[...] + pv_of(p_)
        m_ref[...] = jnp.broadcast_to(m_new, m_ref.shape)

    qbd32 = qbd.astype(F32)
    s = jnp.concatenate([_dot(qbd32, kp[p][0]) for p in range(n)], axis=1) + (fq - _rep_heads(fp_ref[0], nt))

    def pv_pages(p_):
        tot = None
        for p in range(n):
            d = _dot_nt(p_[:, p * PAGE:(p + 1) * PAGE], vp[p][0])
            tot = d if tot is None else tot + d
        return tot

    update(s, pv_pages)

    @pl.when(g == pl.num_programs(1) - 1)
    def _():
        sn = _dot(qbd, kn_ref[0]) + (fq - fnew)
        sn = jnp.where(lane <= tpos, sn, NEG_BIG)
        update(sn, lambda p_: _dot_nt(p_.astype(BF16), vn_ref[0]))
        o_ref[0] = _collapse_heads(acc_ref[...] / l_ref[:, 0:1], nt).astype(BF16)


def _fox_dec(page_table, k_pool, v_pool, qb, knt, vnt, f_past, f_new):
    nb, n_pages = page_table.shape
    nt = qb.shape[1]
    n = PAGES_PER_STEP
    groups = n_pages // n
    rows = N_HEADS * nt
    seq = lambda b, g, pt: (b, 0, 0)
    return pl.pallas_call(
        functools.partial(_fox_dec_kernel, nt=nt),
        grid_spec=pltpu.PrefetchScalarGridSpec(
            num_scalar_prefetch=1,
            grid=(nb, groups),
            in_specs=_page_specs((1, D_HALF, PAGE), n) + _page_specs((1, D_HALF, PAGE), n)
            + [pl.BlockSpec((1, nt, D_HALF), seq), pl.BlockSpec((1, D_HALF, PAGE), seq),
               pl.BlockSpec((1, D_HALF, PAGE), seq),
               pl.BlockSpec((1, N_HEADS, n * PAGE), lambda b, g, pt: (b, 0, g)),
               pl.BlockSpec((1, N_HEADS, PAGE), seq)],
            out_specs=pl.BlockSpec((1, nt, D_HALF), seq),
            scratch_shapes=[pltpu.VMEM((rows, LANES), F32), pltpu.VMEM((rows, LANES), F32),
                            pltpu.VMEM((rows, D_HALF), F32)]),
        out_shape=jax.ShapeDtypeStruct((nb, nt, D_HALF), BF16),
        compiler_params=_params("parallel", "arbitrary"),
        name="fox_decode",
    )(page_table, *([k_pool] * n), *([v_pool] * n), qb, knt, vnt, f_past, f_new)


def _sb_block(qs, kt, vt, run, acc, strict_tri, mask):
    z = _dot(qs, kt)
    lk = -(jnp.maximum(z, 0.0) + jnp.log(1.0 + jnp.exp(-jnp.abs(z))))
    if mask is not None:
        lk = jnp.where(mask, lk, 0.0)
    hi = lk.astype(BF16)
    lo = (lk - hi.astype(F32)).astype(BF16)
    later = _dot(hi, strict_tri) + _dot(lo, strict_tri) + run
    a = jnp.exp(z + lk + later)
    if mask is not None:
        a = jnp.where(mask, a, 0.0)
    acc = acc + _dot_nt(a.astype(BF16), vt)
    run = run + jnp.sum(lk, axis=1, keepdims=True)
    return run, acc


def _sb_dec_kernel(pt_ref, q_ref, kn_ref, vn_ref, kpool_ref, vpool_ref, o_ref, kbuf, vbuf, sem, *, nt, n_pages, gp):
    b = pl.program_id(0)
    slot = b % 2
    rows = N_HEADS * nt
    n_groups = n_pages // gp
    tri = _tri(PAGE, "row_gt_col").astype(BF16)
    qbd = _block_diag_q(q_ref[0])

    def group_copies(seq, grp, slot_):
        out = []
        for p in range(gp):
            page = pt_ref[seq, n_pages - (grp + 1) * gp + p]
            out.append(pltpu.make_async_copy(kpool_ref.at[page], kbuf.at[slot_, p], sem.at[slot_, 0]))
            out.append(pltpu.make_async_copy(vpool_ref.at[page], vbuf.at[slot_, p], sem.at[slot_, 1]))
        return out

    def visit_group(slot_, run, acc):
        for p in reversed(range(gp)):
            run, acc = _sb_block(qbd, kbuf[slot_, p].astype(BF16), vbuf[slot_, p].astype(BF16), run, acc, tri, None)
        return run, acc

    @pl.when(b == 0)
    def _():
        for c in group_copies(0, 0, 0):
            c.start()

    @pl.when(b + 1 < pl.num_programs(0))
    def _():
        for c in group_copies(b + 1, 0, 1 - slot):
            c.start()

    tpos = lax.broadcasted_iota(jnp.int32, (rows, LANES), 0) % nt
    lane = lax.broadcasted_iota(jnp.int32, (rows, LANES), 1)
    run, acc = _sb_block(qbd, kn_ref[0], vn_ref[0], jnp.zeros((rows, 1), F32), jnp.zeros((rows, D_HALF), F32),
                         tri, lane < tpos)

    for c in group_copies(b, 0, slot):
        c.wait()
    run, acc = visit_group(slot, run, acc)

    def cond(c):
        g, run, _ = c
        return (g < n_groups) & (jnp.max(run) > SB_LOG_FLOOR)

    def body(c):
        g, run, acc = c
        copies = group_copies(b, g, slot)
        for cp in copies:
            cp.start()
        for cp in copies:
            cp.wait()
        run, acc = visit_group(slot, run, acc)
        return g + 1, run, acc

    _, _, acc = lax.while_loop(cond, body, (jnp.int32(1), run, acc))
    o_ref[0] = _collapse_heads(acc, nt).astype(BF16)


def _sb_dec(page_table, k_pool, v_pool, qb, knt, vnt):
    nb, n_pages = page_table.shape
    nt = qb.shape[1]
    gp = _row_tile(n_pages, SB_PAGES_PER_FETCH)
    seq = lambda b, pt: (b, 0, 0)
    return pl.pallas_call(
        functools.partial(_sb_dec_kernel, nt=nt, n_pages=n_pages, gp=gp),
        grid_spec=pltpu.PrefetchScalarGridSpec(
            num_scalar_prefetch=1,
            grid=(nb,),
            in_specs=[pl.BlockSpec((1, nt, D_HALF), seq), pl.BlockSpec((1, D_HALF, PAGE), seq),
                      pl.BlockSpec((1, D_HALF, PAGE), seq),
                      pl.BlockSpec(memory_space=pl.ANY), pl.BlockSpec(memory_space=pl.ANY)],
            out_specs=pl.BlockSpec((1, nt, D_HALF), seq),
            scratch_shapes=[pltpu.VMEM((2, gp, D_HALF, PAGE), F32), pltpu.VMEM((2, gp, D_HALF, PAGE), F32),
                            pltpu.SemaphoreType.DMA((2, 2))]),
        out_shape=jax.ShapeDtypeStruct((nb, nt, D_HALF), BF16),
        compiler_params=_params("arbitrary"),
        name="sb_decode",
    )(page_table, qb, knt, vnt, k_pool, v_pool)


def _new_feat_major(kb, b, t):
    return jnp.pad(kb.reshape(b, t, D_HALF).transpose(0, 2, 1), ((0, 0), (0, 0), (0, PAGE - t)))


def _heads_last(kt, b, t):
    return kt.reshape(b, N_HEADS, HEAD_DIM, t).transpose(0, 3, 1, 2)[None]


def _trunk(x, p, cache):
    b, t, d = x.shape
    m = b * t
    x2 = x.reshape(m, d)
    prompt = cache is None

    proj = _proj_even(x2, p["g_mix"][0], p["w_in_even"][0], p["b_forget"][0], p["sgu_g"][0], p["sgu_b"][0],
                      b, t, prompt)
    lfc, lfr, u, vn, vnb = proj[5:]
    w_s, b_s = p["w_spatial"][0].astype(F32), p["b_spatial"][0].astype(F32)
    if prompt:
        qt, kb, kt, vt, vtb = proj[:5]
        frow, kcat = _cumf(lfc, lfr, kb, b, t)
        a = _fox_attn(qt, kcat, vtb, frow, b, t)
        mask = np.tril(np.ones((CHUNK, CHUNK), np.float32))
        bias = jnp.repeat(b_s.T, HEAD_DIM, axis=1)
        w_mix = w_s
        even_rows = (_heads_last(kt, b, t), _heads_last(vt, b, t), lfr.transpose(0, 2, 1)[None], None)
    else:
        qb, k, kb, v, vb = proj[:5]
        pt = cache["page_table"]
        lf_new = jnp.pad(lfr.reshape(N_HEADS, b, t).transpose(1, 0, 2), ((0, 0), (0, 0), (0, PAGE - t)))
        f_past, f_new = _cumf_dec(pt, cache["fox_logf_t"], lf_new)
        a = _fox_dec(pt, cache["fox_k"], cache["fox_v"], qb.reshape(b, t, D_HALF),
                     _new_feat_major(kb, b, t), _new_feat_major(vb, b, t), f_past, f_new).reshape(m, D_HALF)
        reps = CHUNK // t
        idx = np.arange(CHUNK)
        mask = ((idx[:, None] // t == idx[None, :] // t) & (idx[None, :] % t <= idx[:, None] % t)).astype(np.float32)
        w_mix = jnp.tile(w_s[:, :t, :t], (1, reps, reps))
        bias = jnp.tile(jnp.repeat(b_s.T[:t], HEAD_DIM, axis=1), (reps, 1))
        hd = lambda z: z.reshape(1, b, t, N_HEADS, HEAD_DIM)
        even_rows = (hd(k), hd(v), lfc[:, :N_HEADS].reshape(1, b, t, N_HEADS), vn.reshape(1, b, t, D_HALF))
    ob = _sgu(vnb, u, w_mix, jnp.asarray(mask), bias)
    x2 = _mix_mlp(x2, a, ob, p["w_out_even"][0], p["g_mlp"][0], p["w_up"][0], p["w_down"][0], p["g_final"], False)

    proj = _proj_odd(x2, p["g_mix"][1], p["w_in_odd"][0], b, t, prompt)
    gb, uc = proj[5:]
    if prompt:
        qt, kb, kt, vt, vtb = proj[:5]
        a = _sb_attn(qt, kb, vtb, b, t)
        prev = jnp.zeros((b, CONV_W - 1, D_HALF), F32)
        kv_rows = (_heads_last(kt, b, t), _heads_last(vt, b, t))
    else:
        qb, k, kb, v, vb = proj[:5]
        a = _sb_dec(cache["page_table"], cache["sb_k"], cache["sb_v"], qb.reshape(b, t, D_HALF),
                    _new_feat_major(kb, b, t), _new_feat_major(vb, b, t)).reshape(m, D_HALF)
        prev = cache["conv"]
        kv_rows = (k.reshape(1, b, t, N_HEADS, HEAD_DIM), v.reshape(1, b, t, N_HEADS, HEAD_DIM))
    od, new_conv = _conv(uc.reshape(b, t, D_HALF), gb.reshape(b, t, D_HALF), prev, p["conv_w"][0])
    y = _mix_mlp(x2, a, od.reshape(m, D_HALF), p["w_out_odd"][0], p["g_mlp"][1], p["w_up"][1], p["w_down"][1],
                 p["g_final"], True)
    return y.reshape(b, t, d), even_rows, kv_rows + (new_conv[None],)


def _pool_feat_major(cache):
    n_pool = cache.shape[1]
    return cache[0].transpose(0, 2, 3, 1).reshape(n_pool, D_HALF, PAGE)


def kernel(x_prompt, x_sample, cache_fox_k, cache_fox_v, cache_fox_logf, cache_sb_k, cache_sb_v, state_conv,
           page_table, g_mix, g_mlp, g_final, w_up, w_down, w_in_even, b_forget, sgu_g, sgu_b, w_spatial,
           b_spatial, w_out_even, w_in_odd, conv_w, w_out_odd):
    p = dict(g_mix=g_mix, g_mlp=g_mlp, g_final=g_final, w_up=w_up, w_down=w_down, w_in_even=w_in_even,
             b_forget=b_forget, sgu_g=sgu_g, sgu_b=sgu_b, w_spatial=w_spatial, b_spatial=b_spatial,
             w_out_even=w_out_even, w_in_odd=w_in_odd, conv_w=conv_w, w_out_odd=w_out_odd)
    y_p, (pk, pv, plf, _), (psk, psv, pconv) = _trunk(x_prompt, p, None)
    cache = dict(page_table=page_table,
                 fox_k=_pool_feat_major(cache_fox_k), fox_v=_pool_feat_major(cache_fox_v),
                 fox_logf_t=cache_fox_logf[0].transpose(0, 2, 1),
                 sb_k=_pool_feat_major(cache_sb_k), sb_v=_pool_feat_major(cache_sb_v),
                 conv=state_conv[0])
    y_s, (sk, sv, slf, svn), (ssk, ssv, sconv) = _trunk(x_sample, p, cache)
    return (y_p, y_s, pk, pv, plf, psk, psv, pconv, sk, sv, slf, ssk, ssv, sconv, svn)
```
